```python
import jax, jax.numpy as jnp
from jax import lax
import numpy as np

D_MODEL = 1024
BATCH = 4
SEQ = 4096
DEPTH = 2
DEC_BATCH = 128
DEC_SEQ = 8
PAST_LEN = 16384
PAGE_SIZE = 128

HGRN_HEADS = 4
HGRN_KDIM = 128
HGRN_VDIM = 128
HGRN_CHUNK = 32
DIL_GROUPS = ((128, 1), (512, 4), (2048, 16))
DIL_HEADS = 4
DIL_HDIM = 64
MLA_HEADS = 8
MLA_Q_LORA = 384
MLA_KV_LORA = 256
MLA_NOPE = 64
MLA_ROPE = 32
MLA_VDIM = 64
MLA_QBLOCK = 128
ROPE_THETA = 10000.0
MOBA_HEADS = 8
MOBA_KV_HEADS = 4
MOBA_HDIM = 64
MOBA_BLOCK = 256
MOBA_TOPK = 3
MOBA_QCHUNK = 64
MOE_GROUPS = 4
MOE_EXPERTS_PER_GROUP = 8
MOE_EXPERTS = MOE_GROUPS * MOE_EXPERTS_PER_GROUP
MOE_TOPK = 2
MOE_FF = 512
MOE_ROWBLOCK = 128
DN_ALPHA = (2 * DEPTH) ** 0.25
DN_BETA = (8 * DEPTH) ** -0.25
LN_EPS = 1e-5
RMS_EPS = 1e-6

A_QF = HGRN_HEADS * HGRN_KDIM
A_IV = HGRN_HEADS * HGRN_VDIM
B_W = DIL_HEADS * DIL_HDIM
EVEN_IN = 2 * A_QF + 2 * A_IV + 3 * len(DIL_GROUPS) * B_W
EVEN_OUT = A_IV + B_W
MLA_LAT = MLA_KV_LORA + MLA_ROPE
D_Q = MOBA_HEADS * MOBA_HDIM
D_KV = MOBA_KV_HEADS * MOBA_HDIM
ODD_IN = MLA_Q_LORA + MLA_KV_LORA + MLA_ROPE + D_Q + 2 * D_KV
ODD_OUT = MLA_HEADS * MLA_VDIM + D_Q

kernel_name = 'hybrid_hgrn2_dilated_mla_moba_hmoe_step'

F32 = jnp.float32


def _rms_norm(x, g):
    xf = x.astype(F32)
    y = xf * lax.rsqrt(jnp.mean(xf * xf, axis=-1, keepdims=True) + RMS_EPS) * g.astype(F32)
    return y.astype(x.dtype)


def _layer_norm(x, g, b):
    xf = x.astype(F32)
    xc = xf - jnp.mean(xf, axis=-1, keepdims=True)
    y = xc * lax.rsqrt(jnp.mean(xc * xc, axis=-1, keepdims=True) + LN_EPS)
    return (y * g.astype(F32) + b.astype(F32)).astype(x.dtype)


def _softmax_lse(s):
    m = jnp.max(s, axis=-1, keepdims=True)
    e = jnp.exp(s - m)
    l = jnp.sum(e, axis=-1, keepdims=True)
    return e / l, (m + jnp.log(l))[..., 0]


def _rope(x, pos):
    half = x.shape[-1] // 2
    inv = ROPE_THETA ** (-jnp.arange(half, dtype=F32) / half)
    ang = pos.astype(F32)[:, None] * inv[None, :]
    ang = ang.reshape((1, pos.shape[0]) + (1,) * (x.ndim - 3) + (half,))
    cos, sin = jnp.cos(ang), jnp.sin(ang)
    x1 = x[..., :half].astype(F32)
    x2 = x[..., half:].astype(F32)
    return jnp.concatenate([x1 * cos - x2 * sin, x1 * sin + x2 * cos], axis=-1).astype(x.dtype)


def _hgrn2_scan(q, log_f, k, v, s0):
    out_dtype = v.dtype
    q, log_f, k, v = (t.astype(F32) for t in (q, log_f, k, v))
    b, l, h, _ = q.shape
    c = HGRN_CHUNK
    n = -(-l // c)
    pad = n * c - l

    def chunks(t):
        t = jnp.pad(t, ((0, 0), (0, pad), (0, 0), (0, 0)))
        return t.reshape(b, n, c, h, t.shape[-1]).transpose(1, 0, 2, 3, 4)

    causal = jnp.tril(jnp.ones((c, c), dtype=bool))

    def step(state, inp):
        qc, lfc, kc, vc = inp
        cum = jnp.cumsum(lfc, axis=1)
        qd = qc * jnp.exp(cum)
        kd = kc * jnp.exp(-cum)
        att = jnp.where(causal, jnp.einsum('bthk,bshk->bhts', qd, kd), 0.0)
        o = jnp.einsum('bthk,bhkv->bthv', qd, state) + jnp.einsum('bhts,bshv->bthv', att, vc)
        last = cum[:, -1]
        kl = kc * jnp.exp(last[:, None] - cum)
        state = jnp.exp(last)[..., None] * state + jnp.einsum('bshk,bshv->bhkv', kl, vc)
        return state, o

    s_fin, o = lax.scan(step, s0.astype(F32), (chunks(q), chunks(log_f), chunks(k), chunks(v)))
    o = o.transpose(1, 0, 2, 3, 4).reshape(b, n * c, h, v.shape[-1])[:, :l]
    return o.astype(out_dtype), s_fin


def _dilated_prompt(q, k, v, window, dil):
    b, s, h, e = q.shape
    w = window // dil
    unit = dil * w
    sp = -(-s // unit) * unit
    ln = sp // dil
    nb = ln // w

    def to_blocks(t):
        t = jnp.pad(t, ((0, 0), (0, sp - s), (0, 0), (0, 0))).reshape(b, ln, dil, h, e)
        return t.transpose(0, 2, 1, 3, 4).reshape(b, dil, nb, w, h, e)

    def with_prev(t):
        prev = jnp.pad(t, ((0, 0), (0, 0), (1, 0), (0, 0), (0, 0), (0, 0)))[:, :, :-1]
        return jnp.concatenate([prev, t], axis=3)

    qb = to_blocks(q)
    kk = with_prev(to_blocks(k))
    vv = with_prev(to_blocks(v))
    sc = jnp.einsum('bdnqhe,bdnkhe->bdnhqk', qb, kk).astype(F32) * e ** -0.5
    a = jnp.arange(w)
    c = jnp.arange(2 * w)
    dist = w + a[:, None] - c[None, :]
    key_sub = jnp.arange(nb)[:, None, None] * w - w + c[None, None, :]
    mask = (dist >= 0)[None] & (dist <= w)[None] & (key_sub >= 0)
    sc = jnp.where(mask[None, None, :, None], sc, -jnp.inf)
    p, lse = _softmax_lse(sc)
    o = jnp.einsum('bdnhqk,bdnkhe->bdnqhe', p.astype(v.dtype), vv)
    o = o.reshape(b, dil, ln, h, e).transpose(0, 2, 1, 3, 4).reshape(b, sp, h, e)[:, :s]
    lse = lse.transpose(0, 1, 2, 4, 3).reshape(b, dil, ln, h).transpose(0, 2, 1, 3).reshape(b, sp, h)[:, :s]
    return o, lse


def _dilated_sample(q, k, v, buf, window, dil):
    bd, t, h, e = q.shape
    lb = buf.shape[1]
    w = window // dil
    kk = jnp.concatenate([buf[:, :, 0], k], axis=1)
    vv = jnp.concatenate([buf[:, :, 1], v], axis=1)
    idx = lb + jnp.arange(t)[:, None] - dil * jnp.arange(w + 1)[None, :]
    valid = idx >= 0
    idx = jnp.maximum(idx, 0)
    kg = kk[:, idx]
    vg = vv[:, idx]
    sc = jnp.einsum('bthe,btjhe->bthj', q, kg).astype(F32) * e ** -0.5
    sc = jnp.where(valid[None, :, None, :], sc, -jnp.inf)
    p, lse = _softmax_lse(sc)
    o = jnp.einsum('bthj,btjhe->bthe', p.astype(v.dtype), vg)
    cat = jnp.concatenate([buf, jnp.stack([k, v], axis=2)], axis=1)
    new_len = min(window, lb + t)
    return o, lse, cat[:, cat.shape[1] - new_len:]


def _even_project(x, w_in, lb_logits, layer):
    b, l, _ = x.shape
    u = jnp.einsum('bld,dc->blc', x, w_in)
    sizes = [A_QF, A_QF, A_IV, A_IV] + [B_W] * (3 * len(DIL_GROUPS))
    parts = jnp.split(u, [int(z) for z in np.cumsum(sizes)[:-1]], axis=-1)
    q, f_pre, i, g = parts[:4]
    lb = jnp.cumsum(jax.nn.softmax(lb_logits.astype(F32), axis=0), axis=0)[layer]
    f_pre = f_pre.astype(F32)
    log_f = jnp.log(lb + (1.0 - lb) * jax.nn.sigmoid(f_pre))
    k = (1.0 - lb) * jax.nn.sigmoid(-f_pre)
    heads = lambda t, e: t.reshape(b, l, -1, e)
    mixer_a = (heads(q, HGRN_KDIM), heads(log_f, HGRN_KDIM), heads(k, HGRN_KDIM),
               heads(i, HGRN_VDIM), heads(g, HGRN_VDIM))
    groups = [tuple(heads(parts[4 + 3 * j + c], DIL_HDIM) for c in range(3)) for j in range(len(DIL_GROUPS))]
    return mixer_a, groups


def _even_out(o_a, g, outs, lses, norm_g, w_out):
    b, l = o_a.shape[:2]
    gated = (_rms_norm(o_a, norm_g).astype(F32) * jax.nn.silu(g.astype(F32))).astype(o_a.dtype)
    lse = jnp.stack(lses)
    wts = jnp.exp(lse - jax.nn.logsumexp(lse, axis=0, keepdims=True))
    o_b = jnp.sum(wts[..., None] * jnp.stack(outs).astype(F32), axis=0).astype(o_a.dtype)
    h = jnp.concatenate([gated.reshape(b, l, A_IV), o_b.reshape(b, l, B_W)], axis=-1)
    return jnp.einsum('blc,cd->bld', h, w_out)


def _even_prompt(x, w_in, lb_logits, norm_g, w_out, layer):
    b, s, _ = x.shape
    (q, lf, k, i, g), groups = _even_project(x, w_in, lb_logits, layer)
    o_a, s_fin = _hgrn2_scan(q, lf, k, i, jnp.zeros((b, HGRN_HEADS, HGRN_KDIM, HGRN_VDIM), F32))
    outs, lses, bufs = [], [], []
    for (win, dil), (qg, kg, vg) in zip(DIL_GROUPS, groups):
        o, lse = _dilated_prompt(qg, kg, vg, win, dil)
        outs.append(o)
        lses.append(lse)
        bufs.append(jnp.stack([kg, vg], axis=2)[:, s - min(win, s):])
    return _even_out(o_a, g, outs, lses, norm_g, w_out), s_fin.astype(x.dtype), bufs


def _even_sample(x, state, bufs, w_in, lb_logits, norm_g, w_out, layer):
    (q, lf, k, i, g), groups = _even_project(x, w_in, lb_logits, layer)
    o_a, s_new = _hgrn2_scan(q, lf, k, i, state)
    outs, lses, new_bufs = [], [], []
    for (win, dil), (qg, kg, vg), buf in zip(DIL_GROUPS, groups, bufs):
        o, lse, nbuf = _dilated_sample(qg, kg, vg, buf, win, dil)
        outs.append(o)
        lses.append(lse)
        new_bufs.append(nbuf)
    return _even_out(o_a, g, outs, lses, norm_g, w_out), s_new.astype(state.dtype), new_bufs


def _moba_attend(q, qpos, kb, vb):
    tq, _, e = q.shape
    nb = kb.shape[0]
    grp = MOBA_HEADS // MOBA_KV_HEADS
    qg = q.reshape(tq, MOBA_KV_HEADS, grp, e)
    kmean = jnp.mean(kb.astype(F32), axis=1)
    qblk = qpos // MOBA_BLOCK
    gate = jnp.einsum('tkge,nke->tkgn', qg.astype(F32), kmean)
    past = jnp.arange(nb)[None, :] < qblk[:, None]
    gate = jnp.where(past[:, None, None, :], gate, -jnp.inf)
    ksel = min(MOBA_TOPK, nb)
    _, sel = lax.top_k(gate, ksel)
    own = jnp.broadcast_to(qblk[:, None, None, None], (tq, MOBA_KV_HEADS, grp, 1)).astype(sel.dtype)
    idx = jnp.concatenate([sel, own], axis=-1)
    valid = jnp.concatenate([jnp.arange(ksel)[None, :] < qblk[:, None], jnp.ones((tq, 1), bool)], axis=1)
    hsel = jnp.arange(MOBA_KV_HEADS)[None, :, None, None]
    kg = jnp.transpose(kb, (2, 0, 1, 3))[hsel, idx]
    vg = jnp.transpose(vb, (2, 0, 1, 3))[hsel, idx]
    kpos = idx[..., None] * MOBA_BLOCK + jnp.arange(MOBA_BLOCK)
    mask = valid[:, None, None, :, None] & (kpos <= qpos[:, None, None, None, None])
    sc = jnp.einsum('tkge,tkgjne->tkgjn', qg, kg).astype(F32) * e ** -0.5
    sc = jnp.where(mask, sc, -jnp.inf)
    p = jax.nn.softmax(sc.reshape(tq, MOBA_KV_HEADS, grp, -1), axis=-1).reshape(sc.shape)
    o = jnp.einsum('tkgjn,tkgjne->tkge', p.astype(vb.dtype), vg)
    return o.reshape(tq, MOBA_HEADS, e)


def _moba_prompt(qd, kd, vd):
    b, s, h, e = qd.shape
    nb = -(-s // MOBA_BLOCK)
    pad = nb * MOBA_BLOCK - s
    kb = jnp.pad(kd, ((0, 0), (0, pad), (0, 0), (0, 0))).reshape(b, nb, MOBA_BLOCK, MOBA_KV_HEADS, e)
    vb = jnp.pad(vd, ((0, 0), (0, pad), (0, 0), (0, 0))).reshape(b, nb, MOBA_BLOCK, MOBA_KV_HEADS, e)
    nc = s // MOBA_QCHUNK
    qc = qd.reshape(b * nc, MOBA_QCHUNK, h, e)
    pc = jnp.broadcast_to(jnp.arange(s).reshape(1, nc, MOBA_QCHUNK), (b, nc, MOBA_QCHUNK)).reshape(b * nc, MOBA_QCHUNK)
    bi = jnp.repeat(jnp.arange(b), nc)

    def one(args):
        q_, p_, b_ = args
        return _moba_attend(q_, p_, kb[b_], vb[b_])

    return lax.map(one, (qc, pc, bi)).reshape(b, s, h, e)


def _odd_project(x, pos, w_in, q_norm_g, w_uq, kv_norm_g):
    b, l, _ = x.shape
    u = jnp.einsum('bld,dc->blc', x, w_in)
    sizes = [MLA_Q_LORA, MLA_KV_LORA, MLA_ROPE, D_Q, D_KV, D_KV]
    cq, ckv, kr, qd, kd, vd = jnp.split(u, [int(z) for z in np.cumsum(sizes)[:-1]], axis=-1)
    q = jnp.einsum('blr,rhe->blhe', _rms_norm(cq, q_norm_g), w_uq)
    q_nope = q[..., :MLA_NOPE]
    q_rope = _rope(q[..., MLA_NOPE:], pos)
    latent = jnp.concatenate([_rms_norm(ckv, kv_norm_g), _rope(kr, pos)], axis=-1)
    return (q_nope, q_rope, latent, qd.reshape(b, l, MOBA_HEADS, MOBA_HDIM),
            kd.reshape(b, l, MOBA_KV_HEADS, MOBA_HDIM), vd.reshape(b, l, MOBA_KV_HEADS, MOBA_HDIM))


def _mla_prompt_attention(qf, kf, v):
    b, s, h, e = qf.shape
    nq = s // MLA_QBLOCK
    qb = qf.reshape(b, nq, MLA_QBLOCK, h, e).transpose(1, 0, 2, 3, 4)
    kpos = jnp.arange(s)

    def blk(args):
        qblk, i = args
        qpos = i * MLA_QBLOCK + jnp.arange(MLA_QBLOCK)
        sc = jnp.einsum('bqhe,bkhe->bhqk', qblk, kf).astype(F32) * e ** -0.5
        sc = jnp.where(kpos[None, :] <= qpos[:, None], sc, -jnp.inf)
        p = jax.nn.softmax(sc, axis=-1)
        return jnp.einsum('bhqk,bkhe->bqhe', p.astype(v.dtype), v)

    o = lax.map(blk, (qb, jnp.arange(nq)))
    return o.transpose(1, 0, 2, 3, 4).reshape(b, s, h, v.shape[-1])


def _odd_prompt(x, w_in, q_norm_g, w_uq, kv_norm_g, w_uk, w_uv, w_out):
    b, s, _ = x.shape
    q_nope, q_rope, lat, qd, kd, vd = _odd_project(x, jnp.arange(s), w_in, q_norm_g, w_uq, kv_norm_g)
    c = lat[..., :MLA_KV_LORA]
    kr = lat[..., MLA_KV_LORA:]
    k_nope = jnp.einsum('bsr,rhe->bshe', c, w_uk)
    v = jnp.einsum('bsr,rhe->bshe', c, w_uv)
    qf = jnp.concatenate([q_nope, q_rope], axis=-1)
    kf = jnp.concatenate([k_nope, jnp.broadcast_to(kr[:, :, None], (b, s, MLA_HEADS, MLA_ROPE))], axis=-1)
    o_c = _mla_prompt_attention(qf, kf, v)
    o_d = _moba_prompt(qd, kd, vd)
    h = jnp.concatenate([o_c.reshape(b, s, -1), o_d.reshape(b, s, -1)], axis=-1)
    return jnp.einsum('blc,cd->bld', h, w_out), lat, jnp.stack([kd, vd], axis=2)


def _odd_sample(x, cache_mla, cache_moba_kv, page_table, w_in, q_norm_g, w_uq, kv_norm_g, w_uk, w_uv, w_out):
    bd, t, _ = x.shape
    past = page_table.shape[1] * PAGE_SIZE
    pos = past + jnp.arange(t)
    q_nope, q_rope, lat, qd, kd, vd = _odd_project(x, pos, w_in, q_norm_g, w_uq, kv_norm_g)
    q_lat = jnp.einsum('bthe,rhe->bthr', q_nope, w_uk)
    kv_new = jnp.stack([kd, vd], axis=2)
    causal = jnp.arange(past + t)[None, :] <= pos[:, None]
    scale = (MLA_NOPE + MLA_ROPE) ** -0.5

    def mla_one(args):
        pt, ql, qr, ln = args
        rows = jnp.concatenate([cache_mla[pt].reshape(past, MLA_LAT), ln], axis=0)
        c = rows[:, :MLA_KV_LORA]
        kr = rows[:, MLA_KV_LORA:]
        sc = (jnp.einsum('thr,nr->htn', ql, c) + jnp.einsum('the,ne->htn', qr, kr)).astype(F32) * scale
        p = jax.nn.softmax(jnp.where(causal[None], sc, -jnp.inf), axis=-1)
        return jnp.einsum('htn,nr->thr', p.astype(c.dtype), c)

    o_lat = lax.map(mla_one, (page_table, q_lat, q_rope, lat))
    o_c = jnp.einsum('bthr,rhe->bthe', o_lat, w_uv)
    nb = -(-(past + t) // MOBA_BLOCK)
    pad = nb * MOBA_BLOCK - (past + t)

    def moba_one(args):
        pt, q_, kvn = args
        rows = jnp.concatenate([cache_moba_kv[pt].reshape(past, 2, MOBA_KV_HEADS, MOBA_HDIM), kvn], axis=0)
        rows = jnp.pad(rows, ((0, pad), (0, 0), (0, 0), (0, 0))).reshape(nb, MOBA_BLOCK, 2, MOBA_KV_HEADS, MOBA_HDIM)
        return _moba_attend(q_, pos, rows[:, :, 0], rows[:, :, 1])

    o_d = lax.map(moba_one, (page_table, qd, kv_new))
    h = jnp.concatenate([o_c.reshape(bd, t, -1), o_d.reshape(bd, t, -1)], axis=-1)
    return jnp.einsum('blc,cd->bld', h, w_out), lat, kv_new


def _moe_apply(x, eidx, wts, w_gate, w_up, w_down):
    n, d = x.shape
    kk = eidx.shape[1]
    ne = w_gate.shape[0]
    r = MOE_ROWBLOCK
    na = n * kk
    flat_e = eidx.reshape(-1)
    flat_t = jnp.repeat(jnp.arange(n, dtype=jnp.int32), kk)
    flat_w = wts.reshape(-1)
    order = jnp.argsort(flat_e)
    se, st, sw = flat_e[order], flat_t[order], flat_w[order]
    counts = jnp.bincount(flat_e, length=ne)
    padded = (counts + r - 1) // r * r
    pad_end = jnp.cumsum(padded)
    pad_start = pad_end - padded
    start = jnp.cumsum(counts) - counts
    dest = pad_start[se] + jnp.arange(na) - start[se]
    nblk = -(-(na + ne * (r - 1)) // r)
    row_tok = jnp.full((nblk * r,), n, jnp.int32).at[dest].set(st)
    row_w = jnp.zeros((nblk * r,), F32).at[dest].set(sw)
    blk_e = jnp.minimum(jnp.searchsorted(pad_end, jnp.arange(nblk) * r, side='right'), ne - 1)
    xpad = jnp.concatenate([x, jnp.zeros((1, d), x.dtype)], axis=0)

    def one(args):
        e, tok, w = args
        xb = xpad[tok]
        hdn = jax.nn.silu(xb @ w_gate[e]) * (xb @ w_up[e])
        return (hdn @ w_down[e]) * w[:, None].astype(x.dtype)

    out = lax.map(one, (blk_e, row_tok.reshape(nblk, r), row_w.reshape(nblk, r)))
    return jnp.zeros((n + 1, d), x.dtype).at[row_tok].add(out.reshape(-1, d))[:n]


def _hier_moe(x, wg, bg, we, be, w_gate, w_up, w_down):
    n, _ = x.shape
    xf = x.astype(F32)
    glog = xf @ wg.astype(F32) + bg.astype(F32)
    gsel = jnp.argmax(glog, axis=-1).astype(jnp.int32)
    gw = jnp.take_along_axis(jax.nn.softmax(glog, axis=-1), gsel[:, None], axis=-1)
    elog = (xf @ we.astype(F32) + be.astype(F32)).reshape(n, MOE_GROUPS, MOE_EXPERTS_PER_GROUP)
    elog = jnp.take_along_axis(elog, gsel[:, None, None], axis=1)[:, 0]
    tv, ti = lax.top_k(elog, MOE_TOPK)
    wts = gw * jax.nn.softmax(tv, axis=-1)
    eidx = gsel[:, None] * MOE_EXPERTS_PER_GROUP + ti.astype(jnp.int32)
    return _moe_apply(x, eidx, wts, w_gate, w_up, w_down)


def setup_inputs(seed: int = 0) -> dict:
    key = jax.random.key(seed)
    keys = iter(jax.random.split(key, 48))

    def nrm(shape, scale=1.0):
        return jax.random.normal(next(keys), shape, F32) * scale

    d = D_MODEL
    n_pages = PAST_LEN // PAGE_SIZE
    n_used = DEC_BATCH * n_pages
    n_pool = n_used + max(1, n_used // 4)
    inp = {}
    inp['x_prompt'] = nrm((BATCH, SEQ, d))
    inp['x_sample'] = nrm((DEC_BATCH, DEC_SEQ, d))
    inp['state_hgrn'] = nrm((DEC_BATCH, HGRN_HEADS, HGRN_KDIM, HGRN_VDIM))
    for win, _ in DIL_GROUPS:
        inp['cache_swa_w%d' % win] = nrm((DEC_BATCH, min(win, PAST_LEN), 2, DIL_HEADS, DIL_HDIM))
    inp['cache_mla'] = nrm((n_pool, PAGE_SIZE, MLA_LAT))
    inp['cache_moba_kv'] = nrm((n_pool, PAGE_SIZE, 2, MOBA_KV_HEADS, MOBA_HDIM))
    inp['page_table'] = jax.random.permutation(next(keys), n_pool)[:n_used].reshape(DEC_BATCH, n_pages).astype(jnp.int32)
    inp['w_in_even'] = nrm((d, EVEN_IN), d ** -0.5)
    inp['hgrn_lb_logits'] = nrm((DEPTH + 1, A_QF), 0.1)
    inp['hgrn_norm_g'] = 1.0 + nrm((HGRN_VDIM,), 0.1)
    inp['w_out_even'] = nrm((EVEN_OUT, d), EVEN_OUT ** -0.5 * DN_BETA)
    inp['w_in_odd'] = nrm((d, ODD_IN), d ** -0.5)
    inp['mla_q_norm_g'] = 1.0 + nrm((MLA_Q_LORA,), 0.1)
    inp['mla_w_uq'] = nrm((MLA_Q_LORA, MLA_HEADS, MLA_NOPE + MLA_ROPE), MLA_Q_LORA ** -0.5)
    inp['mla_kv_norm_g'] = 1.0 + nrm((MLA_KV_LORA,), 0.1)
    inp['mla_w_uk'] = nrm((MLA_KV_LORA, MLA_HEADS, MLA_NOPE), MLA_KV_LORA ** -0.5)
    inp['mla_w_uv'] = nrm((MLA_KV_LORA, MLA_HEADS, MLA_VDIM), MLA_KV_LORA ** -0.5)
    inp['w_out_odd'] = nrm((ODD_OUT, d), ODD_OUT ** -0.5 * DN_BETA)
    inp['ln_mix_g'] = 1.0 + nrm((DEPTH, d), 0.1)
    inp['ln_mix_b'] = nrm((DEPTH, d), 0.02)
    inp['ln_ffn_g'] = 1.0 + nrm((DEPTH, d), 0.1)
    inp['ln_ffn_b'] = nrm((DEPTH, d), 0.02)
    inp['router_group_w'] = nrm((DEPTH, d, MOE_GROUPS), d ** -0.5)
    inp['router_group_b'] = nrm((DEPTH, MOE_GROUPS), 0.01)
    inp['router_expert_w'] = nrm((DEPTH, d, MOE_EXPERTS), d ** -0.5)
    inp['router_expert_b'] = nrm((DEPTH, MOE_EXPERTS), 0.01)
    inp['expert_w_gate'] = nrm((DEPTH, MOE_EXPERTS, d, MOE_FF), d ** -0.5)
    inp['expert_w_up'] = nrm((DEPTH, MOE_EXPERTS, d, MOE_FF), d ** -0.5)
    inp['expert_w_down'] = nrm((DEPTH, MOE_EXPERTS, MOE_FF, d), MOE_FF ** -0.5 * DN_BETA)
    return inp


def reference(x_prompt, x_sample, state_hgrn, cache_swa_w128, cache_swa_w512, cache_swa_w2048,
              cache_mla, cache_moba_kv, page_table, w_in_even, hgrn_lb_logits, hgrn_norm_g, w_out_even,
              w_in_odd, mla_q_norm_g, mla_w_uq, mla_kv_norm_g, mla_w_uk, mla_w_uv, w_out_odd,
              ln_mix_g, ln_mix_b, ln_ffn_g, ln_ffn_b, router_group_w, router_group_b,
              router_expert_w, router_expert_b, expert_w_gate, expert_w_up, expert_w_down):
    xp, xs = x_prompt, x_sample
    n_p = xp.shape[0] * xp.shape[1]
    for layer in range(DEPTH):
        if layer % 2 == 0:
            mp, hgrn_p, swa_p = _even_prompt(xp, w_in_even, hgrn_lb_logits, hgrn_norm_g, w_out_even, layer)
            ms, hgrn_s, swa_s = _even_sample(xs, state_hgrn, [cache_swa_w128, cache_swa_w512, cache_swa_w2048],
                                             w_in_even, hgrn_lb_logits, hgrn_norm_g, w_out_even, layer)
        else:
            mp, mla_p, moba_p = _odd_prompt(xp, w_in_odd, mla_q_norm_g, mla_w_uq, mla_kv_norm_g,
                                            mla_w_uk, mla_w_uv, w_out_odd)
            ms, mla_s, moba_s = _odd_sample(xs, cache_mla, cache_moba_kv, page_table, w_in_odd, mla_q_norm_g,
                                            mla_w_uq, mla_kv_norm_g, mla_w_uk, mla_w_uv, w_out_odd)
        xp = _layer_norm(DN_ALPHA * xp + mp, ln_mix_g[layer], ln_mix_b[layer])
        xs = _layer_norm(DN_ALPHA * xs + ms, ln_mix_g[layer], ln_mix_b[layer])
        flat = jnp.concatenate([xp.reshape(-1, D_MODEL), xs.reshape(-1, D_MODEL)], axis=0)
        f = _hier_moe(flat, router_group_w[layer], router_group_b[layer], router_expert_w[layer],
                      router_expert_b[layer], expert_w_gate[layer], expert_w_up[layer], expert_w_down[layer])
        xp = _layer_norm(DN_ALPHA * xp + f[:n_p].reshape(xp.shape), ln_ffn_g[layer], ln_ffn_b[layer])
        xs = _layer_norm(DN_ALPHA * xs + f[n_p:].reshape(xs.shape), ln_ffn_g[layer], ln_ffn_b[layer])
    return (xp, xs, hgrn_p, hgrn_s, swa_p[0], swa_s[0], swa_p[1], swa_s[1], swa_p[2], swa_s[2],
            mla_p, mla_s, moba_p, moba_s)
```

```python
import functools
import math

import jax
import jax.numpy as jnp
from jax import lax
from jax.experimental import pallas as pl
from jax.experimental.pallas import tpu as pltpu

F32 = jnp.float32
BF16 = jnp.bfloat16

D_MODEL = 1024
DEPTH = 2
PAGE_SIZE = 128
HGRN_HEADS = 4
HGRN_KDIM = 128
HGRN_VDIM = 128
HGRN_CHUNK = 32
DIL_GROUPS = ((128, 1), (512, 4), (2048, 16))
DIL_HEADS = 4
DIL_HDIM = 64
MLA_HEADS = 8
MLA_Q_LORA = 384
MLA_KV_LORA = 256
MLA_NOPE = 64
MLA_ROPE = 32
MLA_VDIM = 64
ROPE_THETA = 10000.0
MOBA_HEADS = 8
MOBA_KV_HEADS = 4
MOBA_HDIM = 64
MOBA_BLOCK = 256
MOBA_TOPK = 3
MOE_GROUPS = 4
MOE_EXPERTS_PER_GROUP = 8
MOE_EXPERTS = MOE_GROUPS * MOE_EXPERTS_PER_GROUP
MOE_TOPK = 2
MOE_FF = 512
DN_ALPHA = (2 * DEPTH) ** 0.25
LN_EPS = 1e-5
RMS_EPS = 1e-6

A_QF = HGRN_HEADS * HGRN_KDIM
A_IV = HGRN_HEADS * HGRN_VDIM
B_W = DIL_HEADS * DIL_HDIM
EVEN_IN = 2 * A_QF + 2 * A_IV + 3 * len(DIL_GROUPS) * B_W
MLA_LAT = MLA_KV_LORA + MLA_ROPE
D_Q = MOBA_HEADS * MOBA_HDIM
D_KV = MOBA_KV_HEADS * MOBA_HDIM

LANES = 128
SUBLANES = 8
MIB = 2 ** 20

NEG_INF = float("-inf")


def _params(semantics, vmem_mib):
    return pltpu.CompilerParams(dimension_semantics=semantics, vmem_limit_bytes=vmem_mib * MIB)


def _nt(a, b):
    return lax.dot_general(a, b, (((1,), (1,)), ((), ())), preferred_element_type=F32)


def _tn(a, b):
    return lax.dot_general(a, b, (((0,), (0,)), ((), ())), preferred_element_type=F32)


def _nn(a, b):
    return jnp.dot(a, b, preferred_element_type=F32)


def _layer_norm_rows(z, g, b):
    zc = z - jnp.mean(z, axis=-1, keepdims=True)
    y = zc * lax.rsqrt(jnp.mean(zc * zc, axis=-1, keepdims=True) + LN_EPS)
    return y * g + b


def _matmul_kernel(x_ref, w_ref, *o_refs, tn):
    xb = x_ref[...].astype(BF16)
    base = 0
    for o_ref in o_refs:
        n = o_ref.shape[1]
        for c0 in range(0, n, tn):
            c1 = min(c0 + tn, n)
            o_ref[:, c0:c1] = _nn(xb, w_ref[:, base + c0:base + c1]).astype(o_ref.dtype)
        base += n


def _matmul(x, w_bf16, *, splits=None, tm=256, tn=512):
    m, k = x.shape
    n = w_bf16.shape[1]
    splits = splits or (n,)
    assert m % tm == 0 and sum(splits) == n and all(s % LANES == 0 for s in splits)
    outs = pl.pallas_call(
        functools.partial(_matmul_kernel, tn=tn),
        grid=(m // tm,),
        in_specs=[pl.BlockSpec((tm, k), lambda i: (i, 0)),
                  pl.BlockSpec((k, n), lambda i: (0, 0))],
        out_specs=[pl.BlockSpec((tm, s), lambda i: (i, 0)) for s in splits],
        out_shape=[jax.ShapeDtypeStruct((m, s), F32) for s in splits],
        compiler_params=_params(("parallel",), 48),
        name="dense_projection",
    )(x, w_bf16)
    return outs[0] if len(splits) == 1 else outs


def _chunk_cumsum(x, c):
    row = lax.broadcasted_iota(jnp.int32, x.shape, 0) & (c - 1)
    y = x
    s = 1
    while s < c:
        y = y + jnp.where(row >= s, pltpu.roll(y, s, 0), 0.0)
        s *= 2
    return y


def _hgrn_kernel(q_ref, f_ref, i_ref, g_ref, lb_ref, ng_ref, s0_ref, o_ref, sfin_ref, st_scr,
                 *, c, sb, nsb):
    t = pl.program_id(1)

    @pl.when(t == 0)
    def _():
        for h in range(HGRN_HEADS):
            st_scr[h] = s0_ref[0, h].T

    nchunk = sb // c
    row = lax.broadcasted_iota(jnp.int32, (sb, sb), 0)
    col = lax.broadcasted_iota(jnp.int32, (sb, sb), 1)
    causal = (col <= row) & ((row // c) == (col // c))
    ng = ng_ref[...]

    def sub(j, carry):
        rows = pl.ds(pl.multiple_of(j * sb, sb), sb)
        for h in range(HGRN_HEADS):
            cs = slice(HGRN_KDIM * h, HGRN_KDIM * (h + 1))
            q = q_ref[rows, cs]
            fp = f_ref[rows, cs]
            v = i_ref[rows, cs]
            g = g_ref[rows, cs]
            lb = lb_ref[:, cs]
            lf = jnp.log(lb + (1.0 - lb) * jax.nn.sigmoid(fp))
            k = (1.0 - lb) * jax.nn.sigmoid(-fp)
            cum = _chunk_cumsum(lf, c)
            last = jnp.concatenate(
                [jnp.broadcast_to(cum[ci * c + c - 1:ci * c + c, :], (c, HGRN_KDIM)) for ci in range(nchunk)],
                axis=0)
            qd = q * jnp.exp(cum)
            kd = k * jnp.exp(-cum)
            kl = k * jnp.exp(last - cum)
            att = jnp.where(causal, _nt(qd, kd), 0.0)
            intra = _nn(att, v)
            st = st_scr[h]
            outs = []
            for ci in range(nchunk):
                r = slice(ci * c, (ci + 1) * c)
                outs.append(intra[r] + _nt(qd[r], st))
                st = st * jnp.exp(last[ci * c:ci * c + 1, :]) + _tn(v[r], kl[r])
            st_scr[h] = st
            o = jnp.concatenate(outs, axis=0) if nchunk > 1 else outs[0]
            y = o * lax.rsqrt(jnp.mean(o * o, axis=-1, keepdims=True) + RMS_EPS) * ng
            o_ref[rows, cs] = (y * (g * jax.nn.sigmoid(g))).astype(o_ref.dtype)
        return carry

    lax.fori_loop(0, nsb, sub, 0)

    @pl.when(t == pl.num_programs(1) - 1)
    def _():
        for h in range(HGRN_HEADS):
            sfin_ref[0, h] = st_scr[h].T


def _hgrn(u, lb, norm_g, s0, *, row0, seq, tb, c):
    nb = s0.shape[0]
    nt = seq // tb
    sb = min(tb, LANES)
    assert seq % tb == 0 and tb % sb == 0 and sb % c == 0 and row0 % tb == 0
    rb0 = row0 // tb
    in_specs = [pl.BlockSpec((tb, A_QF), lambda b, t, j=j: (rb0 + b * nt + t, j)) for j in range(4)]
    in_specs += [pl.BlockSpec((1, A_QF), lambda b, t: (0, 0)),
                 pl.BlockSpec((1, HGRN_VDIM), lambda b, t: (0, 0)),
                 pl.BlockSpec((1, HGRN_HEADS, HGRN_KDIM, HGRN_VDIM), lambda b, t: (b, 0, 0, 0))]
    return pl.pallas_call(
        functools.partial(_hgrn_kernel, c=c, sb=sb, nsb=tb // sb),
        grid=(nb, nt),
        in_specs=in_specs,
        out_specs=[pl.BlockSpec((tb, A_IV), lambda b, t: (b * nt + t, 0)),
                   pl.BlockSpec((1, HGRN_HEADS, HGRN_KDIM, HGRN_VDIM), lambda b, t: (b, 0, 0, 0))],
        out_shape=[jax.ShapeDtypeStruct((nb * seq, A_IV), F32),
                   jax.ShapeDtypeStruct(s0.shape, F32)],
        scratch_shapes=[pltpu.VMEM((HGRN_HEADS, HGRN_VDIM, HGRN_KDIM), F32)],
        compiler_params=_params(("parallel", "arbitrary"), 48),
        name="hgrn2_scan",
    )(u, u, u, u, lb, norm_g, s0)


def _dil_prompt_kernel(q_ref, kc_ref, vc_ref, kp_ref, vp_ref, o_ref, l_ref):
    w = q_ref.shape[0]
    n = pl.program_id(1)
    a = lax.broadcasted_iota(jnp.int32, (w, 2 * w), 0)
    cc = lax.broadcasted_iota(jnp.int32, (w, 2 * w), 1)
    lo = a + jnp.where(n > 0, 0, 2 * w)
    valid = ((cc >= w) & ((cc - w) <= a)) | ((cc < w) & (cc >= lo))
    scale = DIL_HDIM ** -0.5
    outs, lses = [], []
    for h in range(DIL_HEADS):
        cs = slice(DIL_HDIM * h, DIL_HDIM * (h + 1))
        q = q_ref[:, cs].astype(BF16)
        k = jnp.concatenate([kp_ref[:, cs], kc_ref[:, cs]], axis=0).astype(BF16)
        v = jnp.concatenate([vp_ref[:, cs], vc_ref[:, cs]], axis=0).astype(BF16)
        s = jnp.where(valid, _nt(q, k) * scale, NEG_INF)
        m = jnp.max(s, axis=-1, keepdims=True)
        e = jnp.exp(s - m)
        l = jnp.sum(e, axis=-1, keepdims=True)
        outs.append(_nn(e.astype(BF16), v) / l)
        lses.append(jnp.broadcast_to(m + jnp.log(l), (w, DIL_HDIM)))
    o_ref[...] = jnp.concatenate(outs, axis=1)
    l_ref[...] = jnp.concatenate(lses, axis=1)


def _dil_prompt(u, grp, *, n_prompt, seq):
    win, dil = DIL_GROUPS[grp]
    w = win // dil
    n_tok = u.shape[0]
    unit = dil * w
    assert seq % unit == 0 and n_tok % dil == 0 and w == LANES
    nbatch = n_prompt // seq
    nblk = seq // unit
    ncb = EVEN_IN // B_W
    cb0 = (2 * A_QF + 2 * A_IV) // B_W + 3 * grp
    uv = u.reshape(n_tok // dil, dil * EVEN_IN)

    def spec(cb, prev):
        def imap(b, n, r):
            nn_ = jnp.maximum(n - 1, 0) if prev else n
            return (b * nblk + nn_, r * ncb + cb)
        return pl.BlockSpec((w, B_W), imap)

    out_spec = pl.BlockSpec((w, B_W), lambda b, n, r: (b * nblk + n, r))
    o, lse = pl.pallas_call(
        _dil_prompt_kernel,
        grid=(nbatch, nblk, dil),
        in_specs=[spec(cb0, False), spec(cb0 + 1, False), spec(cb0 + 2, False),
                  spec(cb0 + 1, True), spec(cb0 + 2, True)],
        out_specs=[out_spec, out_spec],
        out_shape=[jax.ShapeDtypeStruct((n_prompt // dil, dil * B_W), F32)] * 2,
        compiler_params=_params(("parallel", "arbitrary", "arbitrary"), 32),
        name=f"dilated_prompt_w{win}",
    )(uv, uv, uv, uv, uv)
    return o.reshape(n_prompt, B_W), lse.reshape(n_prompt, B_W)


def _dil_sample_kernel(q_ref, k_ref, v_ref, buf_ref, nbuf_ref, o_ref, l_ref, *, dil):
    lb = buf_ref.shape[1]
    t_new = q_ref.shape[0]
    nbuf_ref[0, 0:lb - t_new, :] = buf_ref[0, t_new:lb, :]
    nbuf_ref[0, lb - t_new:lb, 0:B_W] = k_ref[...]
    nbuf_ref[0, lb - t_new:lb, B_W:2 * B_W] = v_ref[...]

    nrow = DIL_HEADS * t_new
    q = q_ref[...]
    lane_head = lax.broadcasted_iota(jnp.int32, (t_new, B_W), 1) // DIL_HDIM
    qbd = jnp.concatenate([jnp.where(lane_head == h, q, 0.0) for h in range(DIL_HEADS)], axis=0).astype(BF16)
    scale = DIL_HDIM ** -0.5

    k_main = nbuf_ref[0, :, 0:B_W].astype(BF16)
    v_main = nbuf_ref[0, :, B_W:2 * B_W].astype(BF16)
    k_head = buf_ref[0, 0:t_new, 0:B_W].astype(BF16)
    v_head = buf_ref[0, 0:t_new, B_W:2 * B_W].astype(BF16)

    tq = lax.broadcasted_iota(jnp.int32, (nrow, lb), 0) % t_new
    cm = lax.broadcasted_iota(jnp.int32, (nrow, lb), 1)
    dm = cm + t_new - tq
    valid_m = ((dm & (dil - 1)) == 0) & (dm <= lb)
    th = lax.broadcasted_iota(jnp.int32, (nrow, t_new), 0) % t_new
    dh = lax.broadcasted_iota(jnp.int32, (nrow, t_new), 1) - th
    valid_h = (dh >= 0) & ((dh & (dil - 1)) == 0)

    s_m = jnp.where(valid_m, _nt(qbd, k_main) * scale, NEG_INF)
    s_h = jnp.where(valid_h, _nt(qbd, k_head) * scale, NEG_INF)
    m = jnp.maximum(jnp.max(s_m, axis=-1, keepdims=True), jnp.max(s_h, axis=-1, keepdims=True))
    e_m = jnp.exp(s_m - m)
    e_h = jnp.exp(s_h - m)
    l = jnp.sum(e_m, axis=-1, keepdims=True) + jnp.sum(e_h, axis=-1, keepdims=True)
    o_all = (_nn(e_m.astype(BF16), v_main) + _nn(e_h.astype(BF16), v_head)) / l
    lse_all = jnp.broadcast_to(m + jnp.log(l), (nrow, B_W))
    o = jnp.zeros((t_new, B_W), F32)
    lse = jnp.zeros((t_new, B_W), F32)
    for h in range(DIL_HEADS):
        r = slice(h * t_new, (h + 1) * t_new)
        o = jnp.where(lane_head == h, o_all[r], o)
        lse = jnp.where(lane_head == h, lse_all[r], lse)
    o_ref[...] = o
    l_ref[...] = lse


def _dil_sample(u, buf, grp, *, n_prompt, t_new):
    win, dil = DIL_GROUPS[grp]
    nb, lb = buf.shape[0], buf.shape[1]
    assert lb == win and t_new == SUBLANES and n_prompt % t_new == 0
    cb0 = (2 * A_QF + 2 * A_IV) // B_W + 3 * grp
    rb0 = n_prompt // t_new
    bufv = buf.reshape(nb, lb, 2 * B_W)

    def uspec(cb):
        return pl.BlockSpec((t_new, B_W), lambda b: (rb0 + b, cb))

    tok_spec = pl.BlockSpec((t_new, B_W), lambda b: (b, 0))
    nbuf, o, lse = pl.pallas_call(
        functools.partial(_dil_sample_kernel, dil=dil),
        grid=(nb,),
        in_specs=[uspec(cb0), uspec(cb0 + 1), uspec(cb0 + 2),
                  pl.BlockSpec((1, lb, 2 * B_W), lambda b: (b, 0, 0))],
        out_specs=[pl.BlockSpec((1, lb, 2 * B_W), lambda b: (b, 0, 0)), tok_spec, tok_spec],
        out_shape=[jax.ShapeDtypeStruct(bufv.shape, F32),
                   jax.ShapeDtypeStruct((nb * t_new, B_W), F32),
                   jax.ShapeDtypeStruct((nb * t_new, B_W), F32)],
        compiler_params=_params(("parallel",), 48),
        name=f"dilated_sample_w{win}",
    )(u, u, u, bufv)
    return nbuf.reshape(buf.shape), o, lse


def _two_source_specs(width, tm, npb):
    return [pl.BlockSpec((tm, width), lambda i: (jnp.minimum(i, npb - 1), 0)),
            pl.BlockSpec((tm, width), lambda i: (jnp.maximum(i - npb, 0), 0))]


def _even_out_kernel(*refs, npb):
    srcs = refs[:14]
    x_ref, w_ref, g_ref, b_ref, y_ref = refs[14:]

    def compute(ga_ref, os_, ls_):
        lses = [l[...] for l in ls_]
        mx = jnp.maximum(jnp.maximum(lses[0], lses[1]), lses[2])
        es = [jnp.exp(l - mx) for l in lses]
        tot = es[0] + es[1] + es[2]
        ob = (es[0] / tot) * os_[0][...] + (es[1] / tot) * os_[1][...] + (es[2] / tot) * os_[2][...]
        m = _nn(ga_ref[...].astype(BF16), w_ref[0:A_IV, :]) + _nn(ob.astype(BF16), w_ref[A_IV:A_IV + B_W, :])
        y_ref[...] = _layer_norm_rows(DN_ALPHA * x_ref[...] + m, g_ref[...], b_ref[...])

    i = pl.program_id(0)
    for src, cond in ((0, i < npb), (1, i >= npb)):
        @pl.when(cond)
        def _(src=src):
            compute(srcs[src], srcs[2 + src:8:2], srcs[8 + src:14:2])


def _even_out(gated, os_, lses, x, w_bf16, g, b, *, n_prompt, tm=512):
    n = x.shape[0]
    assert n % tm == 0 and n_prompt % tm == 0
    npb = n_prompt // tm
    const = lambda shape: pl.BlockSpec(shape, lambda i: (0, 0))
    in_specs = _two_source_specs(A_IV, tm, npb)
    args = list(gated)
    for pair in list(os_) + list(lses):
        in_specs += _two_source_specs(B_W, tm, npb)
        args += list(pair)
    in_specs += [pl.BlockSpec((tm, D_MODEL), lambda i: (i, 0)), const(w_bf16.shape),
                 const((1, D_MODEL)), const((1, D_MODEL))]
    return pl.pallas_call(
        functools.partial(_even_out_kernel, npb=npb),
        grid=(n // tm,),
        in_specs=in_specs,
        out_specs=pl.BlockSpec((tm, D_MODEL), lambda i: (i, 0)),
        out_shape=jax.ShapeDtypeStruct((n, D_MODEL), F32),
        compiler_params=_params(("arbitrary",), 40),
        name="even_out_deepnorm",
    )(*args, x, w_bf16, g, b)


def _router_kernel(x_ref, w_ref, b_ref, e_ref, p_ref):
    logits = _nn(x_ref[...].astype(BF16), w_ref[...]) + b_ref[...]
    tm = logits.shape[0]
    lane = lax.broadcasted_iota(jnp.int32, logits.shape, 1)
    big = jnp.int32(LANES)
    is_g = lane < MOE_GROUPS
    gl = jnp.where(is_g, logits, NEG_INF)
    gmax = jnp.max(gl, axis=-1, keepdims=True)
    gsel = jnp.min(jnp.where(is_g & (gl == gmax), lane, big), axis=-1, keepdims=True)
    gw = 1.0 / jnp.sum(jnp.where(is_g, jnp.exp(gl - gmax), 0.0), axis=-1, keepdims=True)
    e_id = lane - MOE_GROUPS
    in_grp = (e_id >= gsel * MOE_EXPERTS_PER_GROUP) & (e_id < (gsel + 1) * MOE_EXPERTS_PER_GROUP)
    el = jnp.where(in_grp, logits, NEG_INF)
    v1 = jnp.max(el, axis=-1, keepdims=True)
    i1 = jnp.min(jnp.where(in_grp & (el == v1), e_id, big), axis=-1, keepdims=True)
    el2 = jnp.where(e_id == i1, NEG_INF, el)
    v2 = jnp.max(el2, axis=-1, keepdims=True)
    i2 = jnp.min(jnp.where(in_grp & (e_id != i1) & (el2 == v2), e_id, big), axis=-1, keepdims=True)
    e2 = jnp.exp(v2 - v1)
    den = 1.0 + e2
    w1 = gw * (1.0 / den)
    w2 = gw * (e2 / den)
    e_ref[...] = jnp.where(lane == 0, i1, jnp.where(lane == 1, i2, 0))
    p_ref[...] = jnp.where(lane == 0, w1, jnp.where(lane == 1, w2, 0.0))
    del tm


def _router(x, w_bf16, bias, *, tm=512):
    n = x.shape[0]
    row = lambda dt: pl.BlockSpec((tm, LANES), lambda i: (i, 0))
    return pl.pallas_call(
        _router_kernel,
        grid=(n // tm,),
        in_specs=[pl.BlockSpec((tm, D_MODEL), lambda i: (i, 0)),
                  pl.BlockSpec((D_MODEL, LANES), lambda i: (0, 0)),
                  pl.BlockSpec((1, LANES), lambda i: (0, 0))],
        out_specs=[row(jnp.int32), row(F32)],
        out_shape=[jax.ShapeDtypeStruct((n, LANES), jnp.int32), jax.ShapeDtypeStruct((n, LANES), F32)],
        compiler_params=_params(("parallel",), 32),
        name="moe_router",
    )(x, w_bf16, bias)


def _expert_kernel(blk_e_ref, src_ref, dst_ref, x_hbm, wg_ref, wu_ref, wd_ref, y_hbm,
                   xg, yo, gsem, ssem, *, rows):
    del blk_e_ref
    i = pl.program_id(0)

    def gather(r, c):
        pltpu.make_async_copy(x_hbm.at[pl.ds(src_ref[i, r], 1)], xg.at[pl.ds(r, 1)], gsem).start()
        return c

    lax.fori_loop(0, rows, gather, 0)

    def gwait(r, c):
        pltpu.make_async_copy(x_hbm.at[pl.ds(0, 1)], xg.at[pl.ds(r, 1)], gsem).wait()
        return c

    lax.fori_loop(0, rows, gwait, 0)
    xb = xg[...].astype(BF16)
    hdn = _nn(xb, wg_ref[0])
    hdn = (hdn * jax.nn.sigmoid(hdn)) * _nn(xb, wu_ref[0])
    yo[...] = _nn(hdn.astype(BF16), wd_ref[0])

    def scatter(r, c):
        d = dst_ref[i, r]

        @pl.when(d >= 0)
        def _():
            pltpu.make_async_copy(yo.at[pl.ds(r, 1)], y_hbm.at[pl.ds(d, 1)], ssem).start()
        return c

    lax.fori_loop(0, rows, scatter, 0)

    def swait(r, c):
        @pl.when(dst_ref[i, r] >= 0)
        def _():
            pltpu.make_async_copy(yo.at[pl.ds(r, 1)], y_hbm.at[pl.ds(0, 1)], ssem).wait()
        return c

    lax.fori_loop(0, rows, swait, 0)


def _experts(x, blk_e, src, dst, wg, wu, wd, *, rows):
    n = x.shape[0]
    nblk = src.shape[0]
    wspec = lambda shape: pl.BlockSpec((1,) + shape, lambda i, be, s, d: (be[i], 0, 0))
    grid_spec = pltpu.PrefetchScalarGridSpec(
        num_scalar_prefetch=3,
        grid=(nblk,),
        in_specs=[pl.BlockSpec(memory_space=pl.ANY),
                  wspec((D_MODEL, MOE_FF)), wspec((D_MODEL, MOE_FF)), wspec((MOE_FF, D_MODEL))],
        out_specs=pl.BlockSpec(memory_space=pl.ANY),
        scratch_shapes=[pltpu.VMEM((rows, D_MODEL), F32), pltpu.VMEM((rows, D_MODEL), F32),
                        pltpu.SemaphoreType.DMA(()), pltpu.SemaphoreType.DMA(())],
    )
    return pl.pallas_call(
        functools.partial(_expert_kernel, rows=rows),
        grid_spec=grid_spec,
        out_shape=jax.ShapeDtypeStruct((MOE_TOPK * n, D_MODEL), F32),
        compiler_params=_params(("arbitrary",), 40),
        name="moe_experts",
    )(blk_e, src, dst, x, wg, wu, wd)


def _moe_combine_kernel(y_ref, p_ref, x_ref, g_ref, b_ref, o_ref):
    p = p_ref[...]
    f = y_ref[:, 0:D_MODEL] * p[:, 0:1] + y_ref[:, D_MODEL:2 * D_MODEL] * p[:, 1:2]
    o_ref[...] = _layer_norm_rows(DN_ALPHA * x_ref[...] + f, g_ref[...], b_ref[...])


def _moe_combine(y2, p, x, g, b, *, tm=512):
    n = x.shape[0]
    return pl.pallas_call(
        _moe_combine_kernel,
        grid=(n // tm,),
        in_specs=[pl.BlockSpec((tm, MOE_TOPK * D_MODEL), lambda i: (i, 0)),
                  pl.BlockSpec((tm, LANES), lambda i: (i, 0)),
                  pl.BlockSpec((tm, D_MODEL), lambda i: (i, 0)),
                  pl.BlockSpec((1, D_MODEL), lambda i: (0, 0)),
                  pl.BlockSpec((1, D_MODEL), lambda i: (0, 0))],
        out_specs=pl.BlockSpec((tm, D_MODEL), lambda i: (i, 0)),
        out_shape=jax.ShapeDtypeStruct((n, D_MODEL), F32),
        compiler_params=_params(("parallel",), 40),
        name="moe_combine_deepnorm",
    )(y2.reshape(n, MOE_TOPK * D_MODEL), p, x, g, b)


def _dispatch_plan(eidx, *, rows):
    n = eidx.shape[0]
    na = n * MOE_TOPK
    flat_e = eidx.reshape(-1)
    order = jnp.argsort(flat_e).astype(jnp.int32)
    se = flat_e[order]
    counts = jnp.bincount(flat_e, length=MOE_EXPERTS).astype(jnp.int32)
    padded = (counts + rows - 1) // rows * rows
    pad_end = jnp.cumsum(padded)
    pad_start = pad_end - padded
    start = jnp.cumsum(counts) - counts
    dest = pad_start[se] + jnp.arange(na, dtype=jnp.int32) - start[se]
    nblk = -(-(na + MOE_EXPERTS * (rows - 1)) // rows)
    dst = jnp.full((nblk * rows,), -1, jnp.int32).at[dest].set(order)
    src = jnp.maximum(dst, 0) // MOE_TOPK
    blk_e = jnp.minimum(jnp.searchsorted(pad_end, jnp.arange(nblk, dtype=jnp.int32) * rows, side="right"),
                        MOE_EXPERTS - 1).astype(jnp.int32)
    return blk_e, src.reshape(nblk, rows), dst.reshape(nblk, rows)


def _hier_moe(x, wr_bf16, br, wg, wu, wd, ln_g, ln_b, *, rows=256):
    e_pad, p_pad = _router(x, wr_bf16, br)
    blk_e, src, dst = _dispatch_plan(e_pad[:, :MOE_TOPK], rows=rows)
    y2 = _experts(x, blk_e, src, dst, wg, wu, wd, rows=rows)
    return _moe_combine(y2, p_pad, x, ln_g, ln_b)


def _router_weights(wg, bg, we, be):
    w = jnp.zeros((D_MODEL, LANES), F32).at[:, :MOE_GROUPS].set(wg).at[:, MOE_GROUPS:MOE_GROUPS + MOE_EXPERTS].set(we)
    b = jnp.zeros((1, LANES), F32).at[0, :MOE_GROUPS].set(bg).at[0, MOE_GROUPS:MOE_GROUPS + MOE_EXPERTS].set(be)
    return w.astype(BF16), b


def _even_layer(x, state_hgrn, caches, w_in, lb_logits, norm_g, w_out, ln_g, ln_b, *, n_prompt, seq, t_new):
    n_tok = x.shape[0]
    nbatch = n_prompt // seq
    u = _matmul(x, w_in.astype(BF16))
    lb = jnp.cumsum(jax.nn.softmax(lb_logits.astype(F32), axis=0), axis=0)[0].reshape(1, A_QF)
    ng = norm_g.reshape(1, HGRN_VDIM)
    zeros = jnp.zeros((nbatch, HGRN_HEADS, HGRN_KDIM, HGRN_VDIM), F32)
    gated_p, hgrn_p = _hgrn(u, lb, ng, zeros, row0=0, seq=seq, tb=512, c=HGRN_CHUNK)
    gated_s, hgrn_s = _hgrn(u, lb, ng, state_hgrn, row0=n_prompt, seq=t_new, tb=t_new, c=t_new)
    os_, lses, swa_p, swa_s = [], [], [], []
    cb0 = 2 * A_QF + 2 * A_IV
    for grp, (win, _) in enumerate(DIL_GROUPS):
        o_p, lse_p = _dil_prompt(u, grp, n_prompt=n_prompt, seq=seq)
        nbuf, o_s, lse_s = _dil_sample(u, caches[grp], grp, n_prompt=n_prompt, t_new=t_new)
        os_.append((o_p, o_s))
        lses.append((lse_p, lse_s))
        swa_s.append(nbuf)
        kv = u[:n_prompt, cb0 + 3 * B_W * grp + B_W:cb0 + 3 * B_W * (grp + 1)]
        kv = kv.reshape(nbatch, seq, 2, DIL_HEADS, DIL_HDIM)
        swa_p.append(kv[:, seq - min(win, seq):])
    y = _even_out((gated_p, gated_s), os_, lses, x, w_out.astype(BF16), ln_g.reshape(1, -1), ln_b.reshape(1, -1),
                  n_prompt=n_prompt)
    return y, hgrn_p, hgrn_s, swa_p, swa_s


HEAD_W = LANES
ROPE_LO = MLA_NOPE
ROPE_MID = MLA_NOPE + MLA_ROPE // 2
ROPE_HI = MLA_NOPE + MLA_ROPE


def _rms_rows(x, g):
    return x * lax.rsqrt(jnp.mean(x * x, axis=-1, keepdims=True) + RMS_EPS) * g


def _odd_prep_kernel(u_ref, cos_ref, sin_ref, gq_ref, gkv_ref, wqa_ref, wqb_ref, wk_ref, wv_ref,
                     lat_ref, q_ref, k_ref, v_ref):
    ckv = u_ref[:, 0:MLA_KV_LORA]
    krb = u_ref[:, MLA_KV_LORA:MLA_KV_LORA + HEAD_W]
    cq = u_ref[:, MLA_KV_LORA + HEAD_W:MLA_KV_LORA + HEAD_W + MLA_Q_LORA]
    cos = cos_ref[...]
    sin = sin_ref[...]
    cos_h = jnp.concatenate([cos] * MLA_HEADS, axis=1)
    sin_h = jnp.concatenate([sin] * MLA_HEADS, axis=1)
    cqn = _rms_rows(cq, gq_ref[...]).astype(BF16)
    q_ref[...] = (_nn(cqn, wqa_ref[...]) * cos_h + _nn(cqn, wqb_ref[...]) * sin_h).astype(BF16)
    c = _rms_rows(ckv, gkv_ref[...])
    lane = lax.broadcasted_iota(jnp.int32, krb.shape, 1)
    swapped = jnp.where(lane < ROPE_MID, pltpu.roll(krb, HEAD_W - MLA_ROPE // 2, 1), pltpu.roll(krb, MLA_ROPE // 2, 1))
    rot = krb * cos + swapped * sin
    lat_ref[:, 0:MLA_KV_LORA] = c
    lat_ref[:, MLA_KV_LORA:MLA_LAT] = rot[:, ROPE_LO:ROPE_HI]
    cb = c.astype(BF16)
    k_ref[...] = (_nn(cb, wk_ref[...]) + jnp.concatenate([rot] * MLA_HEADS, axis=1)).astype(BF16)
    v_ref[...] = _nn(cb, wv_ref[...]).astype(BF16)


def _odd_prep(u_a, cos_t, sin_t, gq, gkv, wqa, wqb, wk, wv, *, tm=256):
    n = u_a.shape[0]
    row = lambda w_: pl.BlockSpec((tm, w_), lambda i: (i, 0))
    const = lambda a: pl.BlockSpec(a.shape, lambda i: (0, 0))
    hw = MLA_HEADS * HEAD_W
    return pl.pallas_call(
        _odd_prep_kernel,
        grid=(n // tm,),
        in_specs=[row(u_a.shape[1]), row(HEAD_W), row(HEAD_W), const(gq), const(gkv),
                  const(wqa), const(wqb), const(wk), const(wv)],
        out_specs=[row(MLA_LAT), row(hw), row(hw), row(MLA_HEADS * MLA_VDIM)],
        out_shape=[jax.ShapeDtypeStruct((n, MLA_LAT), F32), jax.ShapeDtypeStruct((n, hw), BF16),
                   jax.ShapeDtypeStruct((n, hw), BF16), jax.ShapeDtypeStruct((n, MLA_HEADS * MLA_VDIM), BF16)],
        compiler_params=_params(("parallel",), 40),
        name="odd_prep",
    )(u_a, cos_t, sin_t, gq, gkv, wqa, wqb, wk, wv)


def _odd_weights(w_in, w_uq, w_uk, w_uv):
    o = 0
    cq_w = w_in[:, o:o + MLA_Q_LORA]; o += MLA_Q_LORA
    ckv_w = w_in[:, o:o + MLA_KV_LORA]; o += MLA_KV_LORA
    kr_w = w_in[:, o:o + MLA_ROPE]; o += MLA_ROPE
    rest = w_in[:, o:]
    zeros = lambda w_: jnp.zeros((D_MODEL, w_), w_in.dtype)
    w_in2 = jnp.concatenate([ckv_w, zeros(ROPE_LO), kr_w, zeros(HEAD_W - ROPE_HI), cq_w, rest], axis=1)

    half = MLA_ROPE // 2
    nope = w_uq[:, :, :MLA_NOPE]
    x1 = w_uq[:, :, MLA_NOPE:MLA_NOPE + half]
    x2 = w_uq[:, :, MLA_NOPE + half:]
    pad = jnp.zeros((MLA_Q_LORA, MLA_HEADS, HEAD_W - ROPE_HI), w_uq.dtype)
    wqa = jnp.concatenate([nope, x1, x2, pad], axis=2).reshape(MLA_Q_LORA, MLA_HEADS * HEAD_W)
    wqb = jnp.concatenate([jnp.zeros_like(nope), x2, x1, pad], axis=2).reshape(MLA_Q_LORA, MLA_HEADS * HEAD_W)
    kpad = jnp.zeros((MLA_KV_LORA, MLA_HEADS, HEAD_W - MLA_NOPE), w_uk.dtype)
    wk = jnp.concatenate([w_uk, kpad], axis=2).reshape(MLA_KV_LORA, MLA_HEADS * HEAD_W)
    wv = w_uv.reshape(MLA_KV_LORA, MLA_HEADS * MLA_VDIM)
    up = jnp.zeros((MLA_HEADS, HEAD_W, MLA_LAT), F32)
    up = up.at[:, :MLA_NOPE, :MLA_KV_LORA].set(jnp.transpose(w_uk, (1, 2, 0)))
    up = up.at[:, ROPE_LO:ROPE_HI, MLA_KV_LORA:].set(jnp.broadcast_to(jnp.eye(MLA_ROPE, dtype=F32), (MLA_HEADS, MLA_ROPE, MLA_ROPE)))
    wvh = jnp.transpose(w_uv, (1, 0, 2))
    return (w_in2.astype(BF16), wqa.astype(BF16), wqb.astype(BF16), wk.astype(BF16), wv.astype(BF16),
            up.astype(BF16), wvh.astype(BF16))


def _rope_tables(pos):
    half = MLA_ROPE // 2
    inv = ROPE_THETA ** (-jnp.arange(half, dtype=F32) / half)
    ang = pos.astype(F32)[:, None] * inv[None, :]
    cos, sin = jnp.cos(ang), jnp.sin(ang)
    n = pos.shape[0]
    cos_t = jnp.concatenate([jnp.ones((n, ROPE_LO), F32), cos, cos, jnp.zeros((n, HEAD_W - ROPE_HI), F32)], axis=1)
    sin_t = jnp.concatenate([jnp.zeros((n, ROPE_LO), F32), -sin, sin, jnp.zeros((n, HEAD_W - ROPE_HI), F32)], axis=1)
    return cos_t, sin_t


BIG_NEG = -1e30


def _mla_flash_kernel(q_ref, k_ref, v_ref, o_ref, m_scr, l_scr, acc_scr):
    qi = pl.program_id(2)
    ki = pl.program_id(3)
    tq, tk = q_ref.shape[0], k_ref.shape[0]
    scale = (MLA_NOPE + MLA_ROPE) ** -0.5

    @pl.when(ki == 0)
    def _():
        m_scr[...] = jnp.full(m_scr.shape, BIG_NEG, F32)
        l_scr[...] = jnp.zeros(l_scr.shape, F32)
        acc_scr[...] = jnp.zeros(acc_scr.shape, F32)

    @pl.when(ki <= qi)
    def _():
        row = lax.broadcasted_iota(jnp.int32, (tq, tk), 0)
        col = lax.broadcasted_iota(jnp.int32, (tq, tk), 1)
        keep = col <= row + jnp.where(ki < qi, tk, 0)
        v = v_ref[...]
        for hh in range(2):
            cs = slice(HEAD_W * hh, HEAD_W * (hh + 1))
            s = jnp.where(keep, _nt(q_ref[:, cs], k_ref[:, cs]) * scale, BIG_NEG)
            m_old = m_scr[hh]
            m_new = jnp.maximum(m_old, jnp.max(s, axis=-1, keepdims=True))
            alpha = jnp.exp(m_old - m_new)
            p = jnp.exp(s - m_new[:, 0:1])
            l_scr[hh] = alpha * l_scr[hh] + jnp.sum(p, axis=-1, keepdims=True)
            acc_scr[hh] = alpha * acc_scr[hh] + _nn(p.astype(BF16), v)
            m_scr[hh] = m_new

    @pl.when(ki == qi)
    def _():
        lane = lax.broadcasted_iota(jnp.int32, (tq, 2 * MLA_VDIM), 1)
        o = jnp.where(lane < MLA_VDIM, acc_scr[0] / l_scr[0], acc_scr[1] / l_scr[1])
        o_ref[...] = o.astype(o_ref.dtype)


def _mla_prompt(q, k, v, *, n_prompt, seq, tq=512):
    nbatch = n_prompt // seq
    nq = seq // tq
    npair = MLA_HEADS // 2
    return pl.pallas_call(
        _mla_flash_kernel,
        grid=(nbatch, npair, nq, nq),
        in_specs=[pl.BlockSpec((tq, 2 * HEAD_W), lambda b, h, i, j: (b * nq + i, h)),
                  pl.BlockSpec((tq, 2 * HEAD_W), lambda b, h, i, j: (b * nq + jnp.minimum(i, j), h)),
                  pl.BlockSpec((tq, 2 * MLA_VDIM), lambda b, h, i, j: (b * nq + jnp.minimum(i, j), h))],
        out_specs=pl.BlockSpec((tq, 2 * MLA_VDIM), lambda b, h, i, j: (b * nq + i, h)),
        out_shape=jax.ShapeDtypeStruct((n_prompt, MLA_HEADS * MLA_VDIM), BF16),
        scratch_shapes=[pltpu.VMEM((2, tq, LANES), F32), pltpu.VMEM((2, tq, LANES), F32),
                        pltpu.VMEM((2, tq, 2 * MLA_VDIM), F32)],
        compiler_params=_params(("parallel", "parallel", "parallel", "arbitrary"), 32),
        name="mla_prompt_attention",
    )(q, k, v)


def _absorb_kernel(q_ref, up_ref, o_ref):
    for h in range(MLA_HEADS):
        o_ref[h] = _nn(q_ref[:, HEAD_W * h:HEAD_W * (h + 1)], up_ref[h])


def _absorb(q, up, *, row0, nrows, tm=128):
    assert row0 % tm == 0 and nrows % tm == 0
    return pl.pallas_call(
        _absorb_kernel,
        grid=(nrows // tm,),
        in_specs=[pl.BlockSpec((tm, MLA_HEADS * HEAD_W), lambda i: (row0 // tm + i, 0)),
                  pl.BlockSpec(up.shape, lambda i: (0, 0, 0))],
        out_specs=pl.BlockSpec((MLA_HEADS, tm, MLA_LAT), lambda i: (0, i, 0)),
        out_shape=jax.ShapeDtypeStruct((MLA_HEADS, nrows, MLA_LAT), F32),
        compiler_params=_params(("parallel",), 32),
        name="mla_absorb_query",
    )(q, up)


def _page_copy(cache_hbm, buf, sem, page, slot, j, rows):
    return pltpu.make_async_copy(cache_hbm.at[page], buf.at[slot, pl.ds(j * rows, rows)], sem.at[slot])


def _mla_sample_kernel(pt_ref, q_ref, new_ref, cache_hbm, o_ref, buf, sem, *, pg, nchunk):
    b = pl.program_id(0)
    t_new = new_ref.shape[0]
    nrow = MLA_HEADS * t_new
    scale = (MLA_NOPE + MLA_ROPE) ** -0.5

    def start(chunk, slot):
        for j in range(pg):
            _page_copy(cache_hbm, buf, sem, pt_ref[b, chunk * pg + j], slot, j, PAGE_SIZE).start()

    def wait(slot):
        for j in range(pg):
            _page_copy(cache_hbm, buf, sem, 0, slot, j, PAGE_SIZE).wait()

    start(0, 0)
    q = q_ref[...].reshape(nrow, MLA_LAT).astype(BF16)

    def body(c, carry):
        m, l, acc = carry
        slot = c % 2

        @pl.when(c + 1 < nchunk)
        def _():
            start(c + 1, 1 - slot)

        wait(slot)
        kc = buf[slot].astype(BF16)
        s = _nt(q, kc) * scale
        m_new = jnp.maximum(m, jnp.max(s, axis=-1, keepdims=True))
        alpha = jnp.exp(m - m_new)
        p = jnp.exp(s - m_new)
        l = alpha * l + jnp.sum(p, axis=-1, keepdims=True)
        acc = alpha * acc + _nn(p.astype(BF16), kc[:, 0:MLA_KV_LORA])
        return m_new, l, acc

    init = (jnp.full((nrow, 1), BIG_NEG, F32), jnp.zeros((nrow, 1), F32), jnp.zeros((nrow, MLA_KV_LORA), F32))
    m, l, acc = lax.fori_loop(0, nchunk, body, init)

    kn = new_ref[...].astype(BF16)
    tq = lax.broadcasted_iota(jnp.int32, (nrow, t_new), 0) % t_new
    tk = lax.broadcasted_iota(jnp.int32, (nrow, t_new), 1)
    s = jnp.where(tk <= tq, _nt(q, kn) * scale, BIG_NEG)
    m_new = jnp.maximum(m, jnp.max(s, axis=-1, keepdims=True))
    alpha = jnp.exp(m - m_new)
    p = jnp.exp(s - m_new)
    l = alpha * l + jnp.sum(p, axis=-1, keepdims=True)
    acc = alpha * acc + _nn(p.astype(BF16), kn[:, 0:MLA_KV_LORA])
    o_ref[...] = (acc / l).reshape(MLA_HEADS, t_new, MLA_KV_LORA)


def _mla_sample(q_abs, lat, cache, page_table, *, row0, t_new, pg=16):
    nb, npages = page_table.shape
    assert npages % pg == 0 and t_new == SUBLANES and row0 % t_new == 0
    grid_spec = pltpu.PrefetchScalarGridSpec(
        num_scalar_prefetch=1,
        grid=(nb,),
        in_specs=[pl.BlockSpec((MLA_HEADS, t_new, MLA_LAT), lambda b, pt: (0, b, 0)),
                  pl.BlockSpec((t_new, MLA_LAT), lambda b, pt: (row0 // t_new + b, 0)),
                  pl.BlockSpec(memory_space=pl.ANY)],
        out_specs=pl.BlockSpec((MLA_HEADS, t_new, MLA_KV_LORA), lambda b, pt: (0, b, 0)),
        scratch_shapes=[pltpu.VMEM((2, pg * PAGE_SIZE, MLA_LAT), F32), pltpu.SemaphoreType.DMA((2,))],
    )
    return pl.pallas_call(
        functools.partial(_mla_sample_kernel, pg=pg, nchunk=npages // pg),
        grid_spec=grid_spec,
        out_shape=jax.ShapeDtypeStruct((MLA_HEADS, nb * t_new, MLA_KV_LORA), F32),
        compiler_params=_params(("arbitrary",), 40),
        name="mla_sample_attention",
    )(page_table, q_abs, lat, cache)


def _latent_up_kernel(o_ref, w_ref, y_ref):
    outs = [_nn(o_ref[h].astype(BF16), w_ref[h]) for h in range(MLA_HEADS)]
    y_ref[...] = jnp.concatenate(outs, axis=1).astype(y_ref.dtype)


def _latent_up(o_lat, wvh, *, tm=128):
    nrows = o_lat.shape[1]
    return pl.pallas_call(
        _latent_up_kernel,
        grid=(nrows // tm,),
        in_specs=[pl.BlockSpec((MLA_HEADS, tm, MLA_KV_LORA), lambda i: (0, i, 0)),
                  pl.BlockSpec(wvh.shape, lambda i: (0, 0, 0))],
        out_specs=pl.BlockSpec((tm, MLA_HEADS * MLA_VDIM), lambda i: (i, 0)),
        out_shape=jax.ShapeDtypeStruct((nrows, MLA_HEADS * MLA_VDIM), BF16),
        compiler_params=_params(("parallel",), 32),
        name="mla_latent_up",
    )(o_lat, wvh)


def _block_mean_kernel(k_ref, o_ref, *, nblk):
    k = k_ref[...].reshape(nblk, MOBA_BLOCK, D_KV)
    o_ref[0, 0:nblk, :] = jnp.mean(k, axis=1)
    if nblk < o_ref.shape[1]:
        o_ref[0, nblk:, :] = jnp.zeros((o_ref.shape[1] - nblk, D_KV), F32)


def _block_means(u_b, *, n_prompt, seq):
    nbatch = n_prompt // seq
    nblk = seq // MOBA_BLOCK
    assert seq % MOBA_BLOCK == 0 and nblk <= LANES
    return pl.pallas_call(
        functools.partial(_block_mean_kernel, nblk=nblk),
        grid=(nbatch,),
        in_specs=[pl.BlockSpec((seq, D_KV), lambda b: (b, D_Q // D_KV))],
        out_specs=pl.BlockSpec((1, LANES, D_KV), lambda b: (b, 0, 0)),
        out_shape=jax.ShapeDtypeStruct((nbatch, LANES, D_KV), F32),
        compiler_params=_params(("parallel",), 32),
        name="moba_block_means",
    )(u_b)


def _top_blocks(gate, limit):
    lane = lax.broadcasted_iota(jnp.int32, gate.shape, 1)
    g = jnp.where(lane < limit, gate, NEG_INF)
    sel = jnp.zeros(gate.shape, F32)
    for j in range(MOBA_TOPK):
        mx = jnp.max(g, axis=-1, keepdims=True)
        idx = jnp.min(jnp.where(g == mx, lane, LANES), axis=-1, keepdims=True)
        hit = lane == idx + jnp.where(j < limit, 0, 2 * LANES)
        sel = jnp.where(hit, 1.0, sel)
        g = jnp.where(lane == idx, NEG_INF, g)
    return sel


def _moba_flash_kernel(q_ref, k_ref, v_ref, km_ref, o_ref, sel_scr, m_scr, l_scr, acc_scr):
    qi = pl.program_id(2)
    ki = pl.program_id(3)
    tq, tk = q_ref.shape[0], k_ref.shape[0]
    scale = MOBA_HDIM ** -0.5
    grp = MOBA_HEADS // MOBA_KV_HEADS
    nh = 2 * grp

    @pl.when(ki == 0)
    def _():
        m_scr[...] = jnp.full(m_scr.shape, BIG_NEG, F32)
        l_scr[...] = jnp.zeros(l_scr.shape, F32)
        acc_scr[...] = jnp.zeros(acc_scr.shape, F32)
        for hh in range(nh):
            kv = hh // grp
            qh = q_ref[:, MOBA_HDIM * hh:MOBA_HDIM * (hh + 1)].astype(BF16)
            km = km_ref[0, :, MOBA_HDIM * kv:MOBA_HDIM * (kv + 1)].astype(BF16)
            sel_scr[hh] = _top_blocks(_nt(qh, km), qi)

    @pl.when(ki <= qi)
    def _():
        row = lax.broadcasted_iota(jnp.int32, (tq, tk), 0)
        col = lax.broadcasted_iota(jnp.int32, (tq, tk), 1)
        causal = col <= row + jnp.where(ki < qi, tk, 0)
        lane = lax.broadcasted_iota(jnp.int32, (tq, LANES), 1)
        own = jnp.where(ki == qi, 1.0, 0.0)
        for hh in range(nh):
            kv = hh // grp
            ks = slice(MOBA_HDIM * kv, MOBA_HDIM * (kv + 1))
            qh = q_ref[:, MOBA_HDIM * hh:MOBA_HDIM * (hh + 1)].astype(BF16)
            picked = jnp.sum(jnp.where(lane == ki, sel_scr[hh], 0.0), axis=-1, keepdims=True) + own
            keep = causal & (picked > 0.5)
            s = jnp.where(keep, _nt(qh, k_ref[:, ks].astype(BF16)) * scale, BIG_NEG)
            m_old = m_scr[hh]
            m_new = jnp.maximum(m_old, jnp.max(s, axis=-1, keepdims=True))
            alpha = jnp.exp(m_old - m_new)
            p = jnp.where(keep, jnp.exp(s - m_new[:, 0:1]), 0.0)
            l_scr[hh] = alpha * l_scr[hh] + jnp.sum(p, axis=-1, keepdims=True)
            acc_scr[hh] = alpha[:, 0:MOBA_HDIM] * acc_scr[hh] + _nn(p.astype(BF16), v_ref[:, ks].astype(BF16))
            m_scr[hh] = m_new

    @pl.when(ki == qi)
    def _():
        o_ref[...] = jnp.concatenate([acc_scr[hh] / l_scr[hh][:, 0:MOBA_HDIM] for hh in range(nh)],
                                     axis=1).astype(o_ref.dtype)


def _moba_prompt(u_b, kmean, *, n_prompt, seq):
    nbatch = n_prompt // seq
    tq = MOBA_BLOCK
    nq = seq // tq
    npair = MOBA_KV_HEADS // 2
    grp = MOBA_HEADS // MOBA_KV_HEADS
    qw = 2 * grp * MOBA_HDIM
    kw = 2 * MOBA_HDIM
    kcb = D_Q // kw
    vcb = (D_Q + D_KV) // kw
    return pl.pallas_call(
        _moba_flash_kernel,
        grid=(nbatch, npair, nq, nq),
        in_specs=[pl.BlockSpec((tq, qw), lambda b, h, i, j: (b * nq + i, h)),
                  pl.BlockSpec((tq, kw), lambda b, h, i, j: (b * nq + jnp.minimum(i, j), kcb + h)),
                  pl.BlockSpec((tq, kw), lambda b, h, i, j: (b * nq + jnp.minimum(i, j), vcb + h)),
                  pl.BlockSpec((1, LANES, kw), lambda b, h, i, j: (b, 0, h))],
        out_specs=pl.BlockSpec((tq, qw), lambda b, h, i, j: (b * nq + i, h)),
        out_shape=jax.ShapeDtypeStruct((n_prompt, D_Q), BF16),
        scratch_shapes=[pltpu.VMEM((2 * grp, tq, LANES), F32), pltpu.VMEM((2 * grp, tq, LANES), F32),
                        pltpu.VMEM((2 * grp, tq, LANES), F32), pltpu.VMEM((2 * grp, tq, MOBA_HDIM), F32)],
        compiler_params=_params(("parallel", "parallel", "parallel", "arbitrary"), 32),
        name="moba_prompt_attention",
    )(u_b, u_b, u_b, kmean)


def _moba_sample_kernel(pt_ref, q_ref, kvn_ref, cache_hbm, o_ref, buf, sem, km_scr, m_scr, l_scr, o_scr,
                        *, pg, nchunk):
    b = pl.program_id(0)
    t_new = q_ref.shape[0]
    grp = MOBA_HEADS // MOBA_KV_HEADS
    nrow = MOBA_HEADS * t_new
    scale = MOBA_HDIM ** -0.5
    bpc = pg * PAGE_SIZE // MOBA_BLOCK
    nblk = nchunk * bpc

    def start(chunk, slot):
        for j in range(pg):
            _page_copy(cache_hbm, buf, sem, pt_ref[b, chunk * pg + j], slot, j, PAGE_SIZE).start()

    def wait(slot):
        for j in range(pg):
            _page_copy(cache_hbm, buf, sem, 0, slot, j, PAGE_SIZE).wait()

    start(0, 0)
    q = q_ref[...]
    pieces = []
    for h in range(MOBA_HEADS):
        kv = h // grp
        parts = []
        if kv > 0:
            parts.append(jnp.zeros((t_new, MOBA_HDIM * kv), F32))
        parts.append(q[:, MOBA_HDIM * h:MOBA_HDIM * (h + 1)])
        if kv < MOBA_KV_HEADS - 1:
            parts.append(jnp.zeros((t_new, MOBA_HDIM * (MOBA_KV_HEADS - 1 - kv)), F32))
        pieces.append(jnp.concatenate(parts, axis=1))
    qbd = jnp.concatenate(pieces, axis=0).astype(BF16)
    km_scr[...] = jnp.zeros(km_scr.shape, F32)

    def body(c, carry):
        slot = c % 2

        @pl.when(c + 1 < nchunk)
        def _():
            start(c + 1, 1 - slot)

        wait(slot)
        for j in range(bpc):
            n = c * bpc + j
            rows = slice(j * MOBA_BLOCK, (j + 1) * MOBA_BLOCK)
            kf = buf[slot, rows, 0:D_KV]
            kb = kf.astype(BF16)
            vb = buf[slot, rows, D_KV:2 * D_KV].astype(BF16)
            km_scr[pl.ds(n, 1), :] = jnp.mean(kf, axis=0, keepdims=True)
            s = _nt(qbd, kb) * scale
            m = jnp.max(s, axis=-1, keepdims=True)
            p = jnp.exp(s - m)
            m_scr[n] = jnp.broadcast_to(m, (nrow, LANES))
            l_scr[n] = jnp.broadcast_to(jnp.sum(p, axis=-1, keepdims=True), (nrow, LANES))
            o_scr[n] = _nn(p.astype(BF16), vb)
        return carry

    lax.fori_loop(0, nchunk, body, 0)

    kn = kvn_ref[:, 0:D_KV].astype(BF16)
    vn = kvn_ref[:, D_KV:2 * D_KV].astype(BF16)
    tq = lax.broadcasted_iota(jnp.int32, (nrow, t_new), 0) % t_new
    tk = lax.broadcasted_iota(jnp.int32, (nrow, t_new), 1)
    s_own = jnp.where(tk <= tq, _nt(qbd, kn) * scale, BIG_NEG)
    m_own = jnp.max(s_own, axis=-1, keepdims=True)
    p_own = jnp.exp(s_own - m_own)
    l_own = jnp.sum(p_own, axis=-1, keepdims=True)
    o_own = _nn(p_own.astype(BF16), vn)

    sel = _top_blocks(_nt(qbd, km_scr[...].astype(BF16)), nblk)

    def mx_body(n, mx):
        lane = lax.broadcasted_iota(jnp.int32, (nrow, LANES), 1)
        picked = jnp.sum(jnp.where(lane == n, sel, 0.0), axis=-1, keepdims=True)
        return jnp.maximum(mx, jnp.where(picked > 0.5, m_scr[n], BIG_NEG))

    mx = lax.fori_loop(0, nblk, mx_body, jnp.broadcast_to(m_own, (nrow, LANES)))

    def sum_body(n, carry):
        l, o = carry
        lane = lax.broadcasted_iota(jnp.int32, (nrow, LANES), 1)
        picked = jnp.sum(jnp.where(lane == n, sel, 0.0), axis=-1, keepdims=True)
        w_ = jnp.where(picked > 0.5, jnp.exp(m_scr[n] - mx), 0.0)
        w_kv = jnp.concatenate([w_] * (D_KV // LANES), axis=1)
        return l + w_ * l_scr[n], o + w_kv * o_scr[n]

    w_own = jnp.exp(jnp.broadcast_to(m_own, (nrow, LANES)) - mx)
    init = (w_own * l_own, jnp.concatenate([w_own] * (D_KV // LANES), axis=1) * o_own)
    l, o = lax.fori_loop(0, nblk, sum_body, init)
    o = o / jnp.concatenate([l] * (D_KV // LANES), axis=1)
    outs = []
    for h in range(MOBA_HEADS):
        kv = h // grp
        outs.append(o[h * t_new:(h + 1) * t_new, MOBA_HDIM * kv:MOBA_HDIM * (kv + 1)])
    o_ref[...] = jnp.concatenate(outs, axis=1)


def _moba_sample(u_b, cache, page_table, *, row0, t_new, pg=8):
    nb, npages = page_table.shape
    past = npages * PAGE_SIZE
    assert npages % pg == 0 and past % MOBA_BLOCK == 0 and (pg * PAGE_SIZE) % MOBA_BLOCK == 0
    nblk = past // MOBA_BLOCK
    assert nblk <= LANES and t_new == SUBLANES
    nrow = MOBA_HEADS * t_new
    cachev = cache.reshape(cache.shape[0], PAGE_SIZE, 2 * D_KV)
    rb0 = row0 // t_new
    grid_spec = pltpu.PrefetchScalarGridSpec(
        num_scalar_prefetch=1,
        grid=(nb,),
        in_specs=[pl.BlockSpec((t_new, D_Q), lambda b, pt: (rb0 + b, 0)),
                  pl.BlockSpec((t_new, 2 * D_KV), lambda b, pt: (rb0 + b, D_Q // (2 * D_KV))),
                  pl.BlockSpec(memory_space=pl.ANY)],
        out_specs=pl.BlockSpec((t_new, D_Q), lambda b, pt: (b, 0)),
        scratch_shapes=[pltpu.VMEM((2, pg * PAGE_SIZE, 2 * D_KV), F32), pltpu.SemaphoreType.DMA((2,)),
                        pltpu.VMEM((LANES, D_KV), F32),
                        pltpu.VMEM((nblk, nrow, LANES), F32), pltpu.VMEM((nblk, nrow, LANES), F32),
                        pltpu.VMEM((nblk, nrow, D_KV), F32)],
    )
    return pl.pallas_call(
        functools.partial(_moba_sample_kernel, pg=pg, nchunk=npages // pg),
        grid_spec=grid_spec,
        out_shape=jax.ShapeDtypeStruct((nb * t_new, D_Q), F32),
        compiler_params=_params(("arbitrary",), 40),
        name="moba_sample_attention",
    )(page_table, u_b, u_b, cachev)


def _odd_out_kernel(ocp_ref, ocs_ref, odp_ref, ods_ref, x_ref, w_ref, g_ref, b_ref, y_ref, *, npb):
    half = MLA_HEADS * MLA_VDIM

    def compute(oc_ref, od_ref):
        m = (_nn(oc_ref[...].astype(BF16), w_ref[0:half, :])
             + _nn(od_ref[...].astype(BF16), w_ref[half:half + D_Q, :]))
        y_ref[...] = _layer_norm_rows(DN_ALPHA * x_ref[...] + m, g_ref[...], b_ref[...])

    i = pl.program_id(0)

    @pl.when(i < npb)
    def _():
        compute(ocp_ref, odp_ref)

    @pl.when(i >= npb)
    def _():
        compute(ocs_ref, ods_ref)


def _odd_out(oc, od, x, w_bf16, g, b, *, n_prompt, tm=512):
    n = x.shape[0]
    assert n % tm == 0 and n_prompt % tm == 0
    npb = n_prompt // tm
    const = lambda a: pl.BlockSpec(a.shape, lambda i: (0, 0))
    in_specs = (_two_source_specs(oc[0].shape[1], tm, npb) + _two_source_specs(od[0].shape[1], tm, npb)
                + [pl.BlockSpec((tm, D_MODEL), lambda i: (i, 0)), const(w_bf16), const(g), const(b)])
    return pl.pallas_call(
        functools.partial(_odd_out_kernel, npb=npb),
        grid=(n // tm,),
        in_specs=in_specs,
        out_specs=pl.BlockSpec((tm, D_MODEL), lambda i: (i, 0)),
        out_shape=jax.ShapeDtypeStruct((n, D_MODEL), F32),
        compiler_params=_params(("arbitrary",), 40),
        name="odd_out_deepnorm",
    )(*oc, *od, x, w_bf16, g, b)


def _odd_layer(x, cache_mla, cache_moba, page_table, w_in, gq, w_uq, gkv, w_uk, w_uv, w_out, ln_g, ln_b,
               *, n_prompt, seq, t_new):
    n_tok = x.shape[0]
    n_sample = n_tok - n_prompt
    past = page_table.shape[1] * PAGE_SIZE
    w_in2, wqa, wqb, wk, wv, up, wvh = _odd_weights(w_in, w_uq, w_uk, w_uv)
    wa = MLA_KV_LORA + HEAD_W + MLA_Q_LORA
    u_a, u_b = _matmul(x, w_in2, splits=(wa, D_Q + 2 * D_KV))
    pos = jnp.concatenate([jnp.tile(jnp.arange(seq, dtype=jnp.int32), n_prompt // seq),
                           jnp.tile(past + jnp.arange(t_new, dtype=jnp.int32), n_sample // t_new)])
    cos_t, sin_t = _rope_tables(pos)
    lat, q, k, v = _odd_prep(u_a, cos_t, sin_t, gq.reshape(1, -1), gkv.reshape(1, -1), wqa, wqb, wk, wv)
    oc_p = _mla_prompt(q, k, v, n_prompt=n_prompt, seq=seq)
    q_abs = _absorb(q, up, row0=n_prompt, nrows=n_sample)
    o_lat = _mla_sample(q_abs, lat, cache_mla, page_table, row0=n_prompt, t_new=t_new)
    oc_s = _latent_up(o_lat, wvh)
    kmean = _block_means(u_b, n_prompt=n_prompt, seq=seq)
    od_p = _moba_prompt(u_b, kmean, n_prompt=n_prompt, seq=seq)
    od_s = _moba_sample(u_b, cache_moba, page_table, row0=n_prompt, t_new=t_new)
    w_out_b = w_out.astype(BF16)
    g2, b2 = ln_g.reshape(1, -1), ln_b.reshape(1, -1)
    y = _odd_out((oc_p, oc_s), (od_p, od_s), x, w_out_b, g2, b2, n_prompt=n_prompt)
    kv_new = u_b[:, D_Q:]
    return y, lat, kv_new


def kernel(x_prompt, x_sample, state_hgrn, cache_swa_w128, cache_swa_w512, cache_swa_w2048, cache_mla, cache_moba_kv, page_table, w_in_even, hgrn_lb_logits, hgrn_norm_g, w_out_even, w_in_odd, mla_q_norm_g, mla_w_uq, mla_kv_norm_g, mla_w_uk, mla_w_uv, w_out_odd, ln_mix_g, ln_mix_b, ln_ffn_g, ln_ffn_b, router_group_w, router_group_b, router_expert_w, router_expert_b, expert_w_gate, expert_w_up, expert_w_down):
    nbatch, seq, _ = x_prompt.shape
    nb_s, t_new, _ = x_sample.shape
    n_prompt = nbatch * seq
    x = jnp.concatenate([x_prompt.reshape(n_prompt, D_MODEL), x_sample.reshape(nb_s * t_new, D_MODEL)], axis=0)
    caches = [cache_swa_w128, cache_swa_w512, cache_swa_w2048]
    x, hgrn_p, hgrn_s, swa_p, swa_s = _even_layer(
        x, state_hgrn, caches, w_in_even, hgrn_lb_logits, hgrn_norm_g, w_out_even, ln_mix_g[0], ln_mix_b[0],
        n_prompt=n_prompt, seq=seq, t_new=t_new)

    def moe(x, layer):
        wr, br = _router_weights(router_group_w[layer], router_group_b[layer],
                                 router_expert_w[layer], router_expert_b[layer])
        return _hier_moe(x, wr, br, expert_w_gate[layer].astype(BF16), expert_w_up[layer].astype(BF16),
                         expert_w_down[layer].astype(BF16),
                         ln_ffn_g[layer].reshape(1, -1), ln_ffn_b[layer].reshape(1, -1))

    x = moe(x, 0)
    x, lat, kv_new = _odd_layer(
        x, cache_mla, cache_moba_kv, page_table, w_in_odd, mla_q_norm_g, mla_w_uq, mla_kv_norm_g, mla_w_uk,
        mla_w_uv, w_out_odd, ln_mix_g[1], ln_mix_b[1], n_prompt=n_prompt, seq=seq, t_new=t_new)
    x = moe(x, 1)
    kv_shape = (2, MOBA_KV_HEADS, MOBA_HDIM)
    return (x[:n_prompt].reshape(x_prompt.shape), x[n_prompt:].reshape(x_sample.shape),
            hgrn_p, hgrn_s, swa_p[0], swa_s[0], swa_p[1], swa_s[1], swa_p[2], swa_s[2],
            lat[:n_prompt].reshape(nbatch, seq, MLA_LAT), lat[n_prompt:].reshape(nb_s, t_new, MLA_LAT),
            kv_new[:n_prompt].reshape((nbatch, seq) + kv_shape), kv_new[n_prompt:].reshape((nb_s, t_new) + kv_shape))
```

```python
import functools
import math

import jax
import jax.numpy as jnp
from jax import lax
from jax.experimental import pallas as pl
from jax.experimental.pallas import tpu as pltpu

F32 = jnp.float32
BF16 = jnp.bfloat16

D_MODEL = 1024
DEPTH = 2
PAGE_SIZE = 128
HGRN_HEADS = 4
HGRN_KDIM = 128
HGRN_VDIM = 128
HGRN_CHUNK = 32
DIL_GROUPS = ((128, 1), (512, 4), (2048, 16))
DIL_HEADS = 4
DIL_HDIM = 64
MLA_HEADS = 8
MLA_Q_LORA = 384
MLA_KV_LORA = 256
MLA_NOPE = 64
MLA_ROPE = 32
MLA_VDIM = 64
ROPE_THETA = 10000.0
MOBA_HEADS = 8
MOBA_KV_HEADS = 4
MOBA_HDIM = 64
MOBA_BLOCK = 256
MOBA_TOPK = 3
MOE_GROUPS = 4
MOE_EXPERTS_PER_GROUP = 8
MOE_EXPERTS = MOE_GROUPS * MOE_EXPERTS_PER_GROUP
MOE_TOPK = 2
MOE_FF = 512
DN_ALPHA = (2 * DEPTH) ** 0.25
LN_EPS = 1e-5
RMS_EPS = 1e-6

A_QF = HGRN_HEADS * HGRN_KDIM
A_IV = HGRN_HEADS * HGRN_VDIM
B_W = DIL_HEADS * DIL_HDIM
EVEN_IN = 2 * A_QF + 2 * A_IV + 3 * len(DIL_GROUPS) * B_W
MLA_LAT = MLA_KV_LORA + MLA_ROPE
D_Q = MOBA_HEADS * MOBA_HDIM
D_KV = MOBA_KV_HEADS * MOBA_HDIM

LANES = 128
SUBLANES = 8
MIB = 2 ** 20

NEG_INF = float("-inf")


def _params(semantics, vmem_mib):
    return pltpu.CompilerParams(dimension_semantics=semantics, vmem_limit_bytes=vmem_mib * MIB)


def _nt(a, b):
    return lax.dot_general(a, b, (((1,), (1,)), ((), ())), preferred_element_type=F32)


def _tn(a, b):
    return lax.dot_general(a, b, (((0,), (0,)), ((), ())), preferred_element_type=F32)


def _nn(a, b):
    return jnp.dot(a, b, preferred_element_type=F32)


def _layer_norm_rows(z, g, b):
    zc = z - jnp.mean(z, axis=-1, keepdims=True)
    y = zc * lax.rsqrt(jnp.mean(zc * zc, axis=-1, keepdims=True) + LN_EPS)
    return y * g + b


def _matmul_kernel(x_ref, w_ref, *o_refs, tn):
    xb = x_ref[...].astype(BF16)
    base = 0
    for o_ref in o_refs:
        n = o_ref.shape[1]
        for c0 in range(0, n, tn):
            c1 = min(c0 + tn, n)
            o_ref[:, c0:c1] = _nn(xb, w_ref[:, base + c0:base + c1]).astype(o_ref.dtype)
        base += n


def _matmul(x, w_bf16, *, splits=None, tm=256, tn=512):
    m, k = x.shape
    n = w_bf16.shape[1]
    splits = splits or (n,)
    assert m % tm == 0 and sum(splits) == n and all(s % LANES == 0 for s in splits)
    outs = pl.pallas_call(
        functools.partial(_matmul_kernel, tn=tn),
        grid=(m // tm,),
        in_specs=[pl.BlockSpec((tm, k), lambda i: (i, 0)),
                  pl.BlockSpec((k, n), lambda i: (0, 0))],
        out_specs=[pl.BlockSpec((tm, s), lambda i: (i, 0)) for s in splits],
        out_shape=[jax.ShapeDtypeStruct((m, s), F32) for s in splits],
        compiler_params=_params(("parallel",), 48),
        name="dense_projection",
    )(x, w_bf16)
    return outs[0] if len(splits) == 1 else outs


def _chunk_cumsum(x, c):
    row = lax.broadcasted_iota(jnp.int32, x.shape, 0) & (c - 1)
    y = x
    s = 1
    while s < c:
        y = y + jnp.where(row >= s, pltpu.roll(y, s, 0), 0.0)
        s *= 2
    return y


def _hgrn_kernel(q_ref, f_ref, i_ref, g_ref, lb_ref, ng_ref, s0_ref, o_ref, sfin_ref, st_scr,
                 *, c, sb, nsb):
    t = pl.program_id(1)

    @pl.when(t == 0)
    def _():
        for h in range(HGRN_HEADS):
            st_scr[h] = s0_ref[0, h].T

    nchunk = sb // c
    row = lax.broadcasted_iota(jnp.int32, (sb, sb), 0)
    col = lax.broadcasted_iota(jnp.int32, (sb, sb), 1)
    causal = (col <= row) & ((row // c) == (col // c))
    ng = ng_ref[...]

    def sub(j, carry):
        rows = pl.ds(pl.multiple_of(j * sb, sb), sb)
        for h in range(HGRN_HEADS):
            cs = slice(HGRN_KDIM * h, HGRN_KDIM * (h + 1))
            q = q_ref[rows, cs]
            fp = f_ref[rows, cs]
            v = i_ref[rows, cs]
            g = g_ref[rows, cs]
            lb = lb_ref[:, cs]
            lf = jnp.log(lb + (1.0 - lb) * jax.nn.sigmoid(fp))
            k = (1.0 - lb) * jax.nn.sigmoid(-fp)
            cum = _chunk_cumsum(lf, c)
            last = jnp.concatenate(
                [jnp.broadcast_to(cum[ci * c + c - 1:ci * c + c, :], (c, HGRN_KDIM)) for ci in range(nchunk)],
                axis=0)
            qd = q * jnp.exp(cum)
            kd = k * jnp.exp(-cum)
            kl = k * jnp.exp(last - cum)
            att = jnp.where(causal, _nt(qd, kd), 0.0)
            intra = _nn(att, v)
            st = st_scr[h]
            outs = []
            for ci in range(nchunk):
                r = slice(ci * c, (ci + 1) * c)
                outs.append(intra[r] + _nt(qd[r], st))
                st = st * jnp.exp(last[ci * c:ci * c + 1, :]) + _tn(v[r], kl[r])
            st_scr[h] = st
            o = jnp.concatenate(outs, axis=0) if nchunk > 1 else outs[0]
            y = o * lax.rsqrt(jnp.mean(o * o, axis=-1, keepdims=True) + RMS_EPS) * ng
            o_ref[rows, cs] = (y * (g * jax.nn.sigmoid(g))).astype(o_ref.dtype)
        return carry

    lax.fori_loop(0, nsb, sub, 0)

    @pl.when(t == pl.num_programs(1) - 1)
    def _():
        for h in range(HGRN_HEADS):
            sfin_ref[0, h] = st_scr[h].T


def _hgrn(u, lb, norm_g, s0, *, row0, seq, tb, c):
    nb = s0.shape[0]
    nt = seq // tb
    sb = min(tb, LANES)
    assert seq % tb == 0 and tb % sb == 0 and sb % c == 0 and row0 % tb == 0
    rb0 = row0 // tb
    in_specs = [pl.BlockSpec((tb, A_QF), lambda b, t, j=j: (rb0 + b * nt + t, j)) for j in range(4)]
    in_specs += [pl.BlockSpec((1, A_QF), lambda b, t: (0, 0)),
                 pl.BlockSpec((1, HGRN_VDIM), lambda b, t: (0, 0)),
                 pl.BlockSpec((1, HGRN_HEADS, HGRN_KDIM, HGRN_VDIM), lambda b, t: (b, 0, 0, 0))]
    return pl.pallas_call(
        functools.partial(_hgrn_kernel, c=c, sb=sb, nsb=tb // sb),
        grid=(nb, nt),
        in_specs=in_specs,
        out_specs=[pl.BlockSpec((tb, A_IV), lambda b, t: (b * nt + t, 0)),
                   pl.BlockSpec((1, HGRN_HEADS, HGRN_KDIM, HGRN_VDIM), lambda b, t: (b, 0, 0, 0))],
        out_shape=[jax.ShapeDtypeStruct((nb * seq, A_IV), F32),
                   jax.ShapeDtypeStruct(s0.shape, F32)],
        scratch_shapes=[pltpu.VMEM((HGRN_HEADS, HGRN_VDIM, HGRN_KDIM), F32)],
        compiler_params=_params(("parallel", "arbitrary"), 48),
        name="hgrn2_scan",
    )(u, u, u, u, lb, norm_g, s0)


def _dil_prompt_kernel(q_ref, kc_ref, vc_ref, kp_ref, vp_ref, o_ref, l_ref):
    w = q_ref.shape[0]
    n = pl.program_id(1)
    a = lax.broadcasted_iota(jnp.int32, (w, 2 * w), 0)
    cc = lax.broadcasted_iota(jnp.int32, (w, 2 * w), 1)
    lo = a + jnp.where(n > 0, 0, 2 * w)
    valid = ((cc >= w) & ((cc - w) <= a)) | ((cc < w) & (cc >= lo))
    scale = DIL_HDIM ** -0.5
    outs, lses = [], []
    for h in range(DIL_HEADS):
        cs = slice(DIL_HDIM * h, DIL_HDIM * (h + 1))
        q = q_ref[:, cs].astype(BF16)
        k = jnp.concatenate([kp_ref[:, cs], kc_ref[:, cs]], axis=0).astype(BF16)
        v = jnp.concatenate([vp_ref[:, cs], vc_ref[:, cs]], axis=0).astype(BF16)
        s = jnp.where(valid, _nt(q, k) * scale, NEG_INF)
        m = jnp.max(s, axis=-1, keepdims=True)
        e = jnp.exp(s - m)
        l = jnp.sum(e, axis=-1, keepdims=True)
        outs.append(_nn(e.astype(BF16), v) / l)
        lses.append(jnp.broadcast_to(m + jnp.log(l), (w, DIL_HDIM)))
    o_ref[...] = jnp.concatenate(outs, axis=1)
    l_ref[...] = jnp.concatenate(lses, axis=1)


def _dil_prompt(u, grp, *, n_prompt, seq):
    win, dil = DIL_GROUPS[grp]
    w = win // dil
    n_tok = u.shape[0]
    unit = dil * w
    assert seq % unit == 0 and n_tok % dil == 0 and w == LANES
    nbatch = n_prompt // seq
    nblk = seq // unit
    ncb = EVEN_IN // B_W
    cb0 = (2 * A_QF + 2 * A_IV) // B_W + 3 * grp
    uv = u.reshape(n_tok // dil, dil * EVEN_IN)

    def spec(cb, prev):
        def imap(b, n, r):
            nn_ = jnp.maximum(n - 1, 0) if prev else n
            return (b * nblk + nn_, r * ncb + cb)
        return pl.BlockSpec((w, B_W), imap)

    out_spec = pl.BlockSpec((w, B_W), lambda b, n, r: (b * nblk + n, r))
    o, lse = pl.pallas_call(
        _dil_prompt_kernel,
        grid=(nbatch, nblk, dil),
        in_specs=[spec(cb0, False), spec(cb0 + 1, False), spec(cb0 + 2, False),
                  spec(cb0 + 1, True), spec(cb0 + 2, True)],
        out_specs=[out_spec, out_spec],
        out_shape=[jax.ShapeDtypeStruct((n_prompt // dil, dil * B_W), F32)] * 2,
        compiler_params=_params(("parallel", "arbitrary", "arbitrary"), 32),
        name=f"dilated_prompt_w{win}",
    )(uv, uv, uv, uv, uv)
    return o.reshape(n_prompt, B_W), lse.reshape(n_prompt, B_W)


def _dil_sample_kernel(q_ref, k_ref, v_ref, buf_ref, nbuf_ref, o_ref, l_ref, *, dil):
    lb = buf_ref.shape[2]
    t_new = q_ref.shape[0]
    nrow = DIL_HEADS * t_new
    q = q_ref[...]
    lane_head = lax.broadcasted_iota(jnp.int32, (t_new, B_W), 1) // DIL_HDIM
    qbd = jnp.concatenate([jnp.where(lane_head == h, q, 0.0) for h in range(DIL_HEADS)], axis=0).astype(BF16)
    scale = DIL_HDIM ** -0.5

    old = buf_ref[0]
    k_old = old[0:B_W].astype(BF16)
    v_old = old[B_W:2 * B_W].astype(BF16)
    k_new = k_ref[...].astype(BF16)
    v_new = v_ref[...].astype(BF16)

    tq = lax.broadcasted_iota(jnp.int32, (nrow, lb), 0) % t_new
    d_old = lax.broadcasted_iota(jnp.int32, (nrow, lb), 1) - tq
    valid_m = (d_old >= 0) & ((d_old & (dil - 1)) == 0)
    th = lax.broadcasted_iota(jnp.int32, (nrow, t_new), 0) % t_new
    d_new = th - lax.broadcasted_iota(jnp.int32, (nrow, t_new), 1)
    valid_h = (d_new >= 0) & ((d_new & (dil - 1)) == 0)

    s_m = jnp.where(valid_m, _nn(qbd, k_old) * scale, NEG_INF)
    s_h = jnp.where(valid_h, _nt(qbd, k_new) * scale, NEG_INF)
    m = jnp.maximum(jnp.max(s_m, axis=-1, keepdims=True), jnp.max(s_h, axis=-1, keepdims=True))
    e_m = jnp.exp(s_m - m)
    e_h = jnp.exp(s_h - m)
    l = jnp.sum(e_m, axis=-1, keepdims=True) + jnp.sum(e_h, axis=-1, keepdims=True)
    o_all = (_nt(e_m.astype(BF16), v_old) + _nn(e_h.astype(BF16), v_new)) / l

    nbuf_ref[0] = pltpu.roll(old, lb - t_new, 1)
    kv = jnp.concatenate([k_ref[...], v_ref[...]], axis=1)
    kv_t = jnp.concatenate([kv, jnp.zeros((LANES - t_new, 2 * B_W), F32)], axis=0).T
    lane = lax.broadcasted_iota(jnp.int32, (2 * B_W, LANES), 1)
    tail = jnp.where(lane >= LANES - t_new, pltpu.roll(kv_t, LANES - t_new, 1), nbuf_ref[0, :, lb - LANES:lb])
    nbuf_ref[0, :, lb - LANES:lb] = tail
    lse_all = jnp.broadcast_to(m + jnp.log(l), (nrow, B_W))
    o = jnp.zeros((t_new, B_W), F32)
    lse = jnp.zeros((t_new, B_W), F32)
    for h in range(DIL_HEADS):
        r = slice(h * t_new, (h + 1) * t_new)
        o = jnp.where(lane_head == h, o_all[r], o)
        lse = jnp.where(lane_head == h, lse_all[r], lse)
    o_ref[...] = o
    l_ref[...] = lse


def _dil_sample(u, buf, grp, *, n_prompt, t_new):
    win, dil = DIL_GROUPS[grp]
    nb, lb = buf.shape[0], buf.shape[1]
    assert lb == win and t_new == SUBLANES and n_prompt % t_new == 0
    cb0 = (2 * A_QF + 2 * A_IV) // B_W + 3 * grp
    rb0 = n_prompt // t_new
    bufv = jnp.transpose(buf, (0, 2, 3, 4, 1)).reshape(nb, 2 * B_W, lb)

    def uspec(cb):
        return pl.BlockSpec((t_new, B_W), lambda b: (rb0 + b, cb))

    tok_spec = pl.BlockSpec((t_new, B_W), lambda b: (b, 0))
    nbuf, o, lse = pl.pallas_call(
        functools.partial(_dil_sample_kernel, dil=dil),
        grid=(nb,),
        in_specs=[uspec(cb0), uspec(cb0 + 1), uspec(cb0 + 2),
                  pl.BlockSpec((1, 2 * B_W, lb), lambda b: (b, 0, 0))],
        out_specs=[pl.BlockSpec((1, 2 * B_W, lb), lambda b: (b, 0, 0)), tok_spec, tok_spec],
        out_shape=[jax.ShapeDtypeStruct(bufv.shape, F32),
                   jax.ShapeDtypeStruct((nb * t_new, B_W), F32),
                   jax.ShapeDtypeStruct((nb * t_new, B_W), F32)],
        compiler_params=_params(("parallel",), 48),
        name=f"dilated_sample_w{win}",
    )(u, u, u, bufv)
    nbuf = jnp.transpose(nbuf.reshape(nb, 2, DIL_HEADS, DIL_HDIM, lb), (0, 4, 1, 2, 3))
    return nbuf, o, lse


def _two_source_specs(width, tm, npb):
    return [pl.BlockSpec((tm, width), lambda i: (jnp.minimum(i, npb - 1), 0)),
            pl.BlockSpec((tm, width), lambda i: (jnp.maximum(i - npb, 0), 0))]


def _even_out_kernel(*refs, npb):
    srcs = refs[:14]
    x_ref, w_ref, g_ref, b_ref, y_ref = refs[14:]

    def compute(ga_ref, os_, ls_):
        lses = [l[...] for l in ls_]
        mx = jnp.maximum(jnp.maximum(lses[0], lses[1]), lses[2])
        es = [jnp.exp(l - mx) for l in lses]
        tot = es[0] + es[1] + es[2]
        ob = (es[0] / tot) * os_[0][...] + (es[1] / tot) * os_[1][...] + (es[2] / tot) * os_[2][...]
        m = _nn(ga_ref[...].astype(BF16), w_ref[0:A_IV, :]) + _nn(ob.astype(BF16), w_ref[A_IV:A_IV + B_W, :])
        y_ref[...] = _layer_norm_rows(DN_ALPHA * x_ref[...] + m, g_ref[...], b_ref[...])

    i = pl.program_id(0)
    for src, cond in ((0, i < npb), (1, i >= npb)):
        @pl.when(cond)
        def _(src=src):
            compute(srcs[src], srcs[2 + src:8:2], srcs[8 + src:14:2])


def _even_out(gated, os_, lses, x, w_bf16, g, b, *, n_prompt, tm=512):
    n = x.shape[0]
    assert n % tm == 0 and n_prompt % tm == 0
    npb = n_prompt // tm
    const = lambda shape: pl.BlockSpec(shape, lambda i: (0, 0))
    in_specs = _two_source_specs(A_IV, tm, npb)
    args = list(gated)
    for pair in list(os_) + list(lses):
        in_specs += _two_source_specs(B_W, tm, npb)
        args += list(pair)
    in_specs += [pl.BlockSpec((tm, D_MODEL), lambda i: (i, 0)), const(w_bf16.shape),
                 const((1, D_MODEL)), const((1, D_MODEL))]
    return pl.pallas_call(
        functools.partial(_even_out_kernel, npb=npb),
        grid=(n // tm,),
        in_specs=in_specs,
        out_specs=pl.BlockSpec((tm, D_MODEL), lambda i: (i, 0)),
        out_shape=jax.ShapeDtypeStruct((n, D_MODEL), F32),
        compiler_params=_params(("arbitrary",), 40),
        name="even_out_deepnorm",
    )(*args, x, w_bf16, g, b)


def _router_kernel(x_ref, w_ref, b_ref, e_ref, p_ref):
    logits = _nn(x_ref[...].astype(BF16), w_ref[...]) + b_ref[...]
    tm = logits.shape[0]
    lane = lax.broadcasted_iota(jnp.int32, logits.shape, 1)
    big = jnp.int32(LANES)
    is_g = lane < MOE_GROUPS
    gl = jnp.where(is_g, logits, NEG_INF)
    gmax = jnp.max(gl, axis=-1, keepdims=True)
    gsel = jnp.min(jnp.where(is_g & (gl == gmax), lane, big), axis=-1, keepdims=True)
    gw = 1.0 / jnp.sum(jnp.where(is_g, jnp.exp(gl - gmax), 0.0), axis=-1, keepdims=True)
    e_id = lane - MOE_GROUPS
    in_grp = (e_id >= gsel * MOE_EXPERTS_PER_GROUP) & (e_id < (gsel + 1) * MOE_EXPERTS_PER_GROUP)
    el = jnp.where(in_grp, logits, NEG_INF)
    v1 = jnp.max(el, axis=-1, keepdims=True)
    i1 = jnp.min(jnp.where(in_grp & (el == v1), e_id, big), axis=-1, keepdims=True)
    el2 = jnp.where(e_id == i1, NEG_INF, el)
    v2 = jnp.max(el2, axis=-1, keepdims=True)
    i2 = jnp.min(jnp.where(in_grp & (e_id != i1) & (el2 == v2), e_id, big), axis=-1, keepdims=True)
    e2 = jnp.exp(v2 - v1)
    den = 1.0 + e2
    w1 = gw * (1.0 / den)
    w2 = gw * (e2 / den)
    e_ref[...] = jnp.where(lane == 0, i1, jnp.where(lane == 1, i2, 0))
    p_ref[...] = jnp.where(lane == 0, w1, jnp.where(lane == 1, w2, 0.0))
    del tm


def _router(x, w_bf16, bias, *, tm=512):
    n = x.shape[0]
    row = lambda dt: pl.BlockSpec((tm, LANES), lambda i: (i, 0))
    return pl.pallas_call(
        _router_kernel,
        grid=(n // tm,),
        in_specs=[pl.BlockSpec((tm, D_MODEL), lambda i: (i, 0)),
                  pl.BlockSpec((D_MODEL, LANES), lambda i: (0, 0)),
                  pl.BlockSpec((1, LANES), lambda i: (0, 0))],
        out_specs=[row(jnp.int32), row(F32)],
        out_shape=[jax.ShapeDtypeStruct((n, LANES), jnp.int32), jax.ShapeDtypeStruct((n, LANES), F32)],
        compiler_params=_params(("parallel",), 32),
        name="moe_router",
    )(x, w_bf16, bias)


def _expert_kernel(blk_e_ref, src_ref, dst_ref, x_hbm, wg_ref, wu_ref, wd_ref, y_hbm,
                   xg, yo, gsem, ssem, *, rows):
    del blk_e_ref
    i = pl.program_id(0)
    nsteps = pl.num_programs(0)
    slot = i % 2

    def gather_start(step, sl):
        def body(r, c):
            pltpu.make_async_copy(x_hbm.at[pl.ds(src_ref[step, r], 1)], xg.at[sl, pl.ds(r, 1)],
                                  gsem.at[sl]).start()
            return c
        lax.fori_loop(0, rows, body, 0, unroll=8)

    def scatter_start(step, sl):
        def body(r, c):
            pltpu.make_async_copy(yo.at[sl, pl.ds(r, 1)], y_hbm.at[pl.ds(dst_ref[step, r], 1)],
                                  ssem.at[sl]).start()
            return c
        lax.fori_loop(0, rows, body, 0, unroll=8)

    def gather_wait(sl):
        pltpu.make_async_copy(x_hbm.at[pl.ds(0, rows)], xg.at[sl], gsem.at[sl]).wait()

    def scatter_wait(sl):
        pltpu.make_async_copy(yo.at[sl], y_hbm.at[pl.ds(0, rows)], ssem.at[sl]).wait()

    @pl.when(i == 0)
    def _():
        gather_start(0, 0)
        spare = y_hbm.shape[0] - 2 * rows
        for sl in range(2):
            yo[sl] = jnp.zeros((rows, D_MODEL), F32)
            fill = pltpu.make_async_copy(yo.at[sl], y_hbm.at[pl.ds(spare + sl * rows, rows)], ssem.at[sl])
            fill.start()
            fill.wait()

    @pl.when(i + 1 < nsteps)
    def _():
        gather_start(i + 1, 1 - slot)

    gather_wait(slot)

    @pl.when(i >= 2)
    def _():
        scatter_wait(slot)

    xb = xg[slot].astype(BF16)
    hdn = _nn(xb, wg_ref[0])
    hdn = (hdn * jax.nn.sigmoid(hdn)) * _nn(xb, wu_ref[0])
    yo[slot] = _nn(hdn.astype(BF16), wd_ref[0])
    scatter_start(i, slot)

    @pl.when(i == nsteps - 1)
    def _():
        scatter_wait(slot)

        @pl.when(nsteps >= 2)
        def _():
            scatter_wait(1 - slot)


def _experts(x, blk_e, src, dst, wg, wu, wd, *, rows):
    n = x.shape[0]
    nblk = src.shape[0]
    wspec = lambda shape: pl.BlockSpec((1,) + shape, lambda i, be, s, d: (be[i], 0, 0))
    grid_spec = pltpu.PrefetchScalarGridSpec(
        num_scalar_prefetch=3,
        grid=(nblk,),
        in_specs=[pl.BlockSpec(memory_space=pl.ANY),
                  wspec((D_MODEL, MOE_FF)), wspec((D_MODEL, MOE_FF)), wspec((MOE_FF, D_MODEL))],
        out_specs=pl.BlockSpec(memory_space=pl.ANY),
        scratch_shapes=[pltpu.VMEM((2, rows, D_MODEL), F32), pltpu.VMEM((2, rows, D_MODEL), F32),
                        pltpu.SemaphoreType.DMA((2,)), pltpu.SemaphoreType.DMA((2,))],
    )
    return pl.pallas_call(
        functools.partial(_expert_kernel, rows=rows),
        grid_spec=grid_spec,
        out_shape=jax.ShapeDtypeStruct((MOE_TOPK * n + 2 * rows, D_MODEL), F32),
        compiler_params=_params(("arbitrary",), 40),
        name="moe_experts",
    )(blk_e, src, dst, x, wg, wu, wd)


def _moe_combine_kernel(y0_ref, y1_ref, p_ref, x_ref, g_ref, b_ref, o_ref):
    p = p_ref[...]
    f = y0_ref[...] * p[:, 0:1] + y1_ref[...] * p[:, 1:2]
    o_ref[...] = _layer_norm_rows(DN_ALPHA * x_ref[...] + f, g_ref[...], b_ref[...])


def _moe_combine(y2, p, x, g, b, *, tm=512):
    n = x.shape[0]
    assert n % tm == 0
    return pl.pallas_call(
        _moe_combine_kernel,
        grid=(n // tm,),
        in_specs=[pl.BlockSpec((tm, D_MODEL), lambda i: (i, 0)),
                  pl.BlockSpec((tm, D_MODEL), lambda i: (n // tm + i, 0)),
                  pl.BlockSpec((tm, LANES), lambda i: (i, 0)),
                  pl.BlockSpec((tm, D_MODEL), lambda i: (i, 0)),
                  pl.BlockSpec((1, D_MODEL), lambda i: (0, 0)),
                  pl.BlockSpec((1, D_MODEL), lambda i: (0, 0))],
        out_specs=pl.BlockSpec((tm, D_MODEL), lambda i: (i, 0)),
        out_shape=jax.ShapeDtypeStruct((n, D_MODEL), F32),
        compiler_params=_params(("parallel",), 40),
        name="moe_combine_deepnorm",
    )(y2, y2, p, x, g, b)


def _dispatch_plan(eidx, *, rows):
    n = eidx.shape[0]
    na = n * MOE_TOPK
    flat_e = eidx.T.reshape(-1)
    order = jnp.argsort(flat_e).astype(jnp.int32)
    se = flat_e[order]
    counts = jnp.bincount(flat_e, length=MOE_EXPERTS).astype(jnp.int32)
    padded = (counts + rows - 1) // rows * rows
    pad_end = jnp.cumsum(padded)
    pad_start = pad_end - padded
    start = jnp.cumsum(counts) - counts
    dest = pad_start[se] + jnp.arange(na, dtype=jnp.int32) - start[se]
    nblk = -(-(na + MOE_EXPERTS * (rows - 1)) // rows)
    dst = jnp.full((nblk * rows,), -1, jnp.int32).at[dest].set(order)
    pad = dst < 0
    src = jnp.where(pad, 0, dst % n)
    dst = jnp.where(pad, na + jnp.arange(nblk * rows, dtype=jnp.int32) % (2 * rows), dst)
    blk_e = jnp.minimum(jnp.searchsorted(pad_end, jnp.arange(nblk, dtype=jnp.int32) * rows, side="right"),
                        MOE_EXPERTS - 1).astype(jnp.int32)
    return blk_e, src.reshape(nblk, rows), dst.reshape(nblk, rows)


def _hier_moe(x, wr_bf16, br, wg, wu, wd, ln_g, ln_b, *, rows=256):
    e_pad, p_pad = _router(x, wr_bf16, br)
    blk_e, src, dst = _dispatch_plan(e_pad[:, :MOE_TOPK], rows=rows)
    y2 = _experts(x, blk_e, src, dst, wg, wu, wd, rows=rows)
    return _moe_combine(y2, p_pad, x, ln_g, ln_b)


def _router_weights(wg, bg, we, be):
    w = jnp.zeros((D_MODEL, LANES), F32).at[:, :MOE_GROUPS].set(wg).at[:, MOE_GROUPS:MOE_GROUPS + MOE_EXPERTS].set(we)
    b = jnp.zeros((1, LANES), F32).at[0, :MOE_GROUPS].set(bg).at[0, MOE_GROUPS:MOE_GROUPS + MOE_EXPERTS].set(be)
    return w.astype(BF16), b


def _even_layer(x, state_hgrn, caches, w_in, lb_logits, norm_g, w_out, ln_g, ln_b, *, n_prompt, seq, t_new):
    n_tok = x.shape[0]
    nbatch = n_prompt // seq
    u = _matmul(x, w_in.astype(BF16))
    lb = jnp.cumsum(jax.nn.softmax(lb_logits.astype(F32), axis=0), axis=0)[0].reshape(1, A_QF)
    ng = norm_g.reshape(1, HGRN_VDIM)
    zeros = jnp.zeros((nbatch, HGRN_HEADS, HGRN_KDIM, HGRN_VDIM), F32)
    gated_p, hgrn_p = _hgrn(u, lb, ng, zeros, row0=0, seq=seq, tb=512, c=HGRN_CHUNK)
    gated_s, hgrn_s = _hgrn(u, lb, ng, state_hgrn, row0=n_prompt, seq=t_new, tb=t_new, c=t_new)
    os_, lses, swa_p, swa_s = [], [], [], []
    cb0 = 2 * A_QF + 2 * A_IV
    for grp, (win, _) in enumerate(DIL_GROUPS):
        o_p, lse_p = _dil_prompt(u, grp, n_prompt=n_prompt, seq=seq)
        nbuf, o_s, lse_s = _dil_sample(u, caches[grp], grp, n_prompt=n_prompt, t_new=t_new)
        os_.append((o_p, o_s))
        lses.append((lse_p, lse_s))
        swa_s.append(nbuf)
        kv = u[:n_prompt, cb0 + 3 * B_W * grp + B_W:cb0 + 3 * B_W * (grp + 1)]
        kv = kv.reshape(nbatch, seq, 2, DIL_HEADS, DIL_HDIM)
        swa_p.append(kv[:, seq - min(win, seq):])
    y = _even_out((gated_p, gated_s), os_, lses, x, w_out.astype(BF16), ln_g.reshape(1, -1), ln_b.reshape(1, -1),
                  n_prompt=n_prompt)
    return y, hgrn_p, hgrn_s, swa_p, swa_s


HEAD_W = LANES
ROPE_LO = MLA_NOPE
ROPE_MID = MLA_NOPE + MLA_ROPE // 2
ROPE_HI = MLA_NOPE + MLA_ROPE


def _rms_rows(x, g):
    return x * lax.rsqrt(jnp.mean(x * x, axis=-1, keepdims=True) + RMS_EPS) * g


def _odd_prep_kernel(u_ref, cos_ref, sin_ref, gq_ref, gkv_ref, wqa_ref, wqb_ref, wk_ref, wv_ref,
                     lat_ref, q_ref, k_ref, v_ref):
    ckv = u_ref[:, 0:MLA_KV_LORA]
    krb = u_ref[:, MLA_KV_LORA:MLA_KV_LORA + HEAD_W]
    cq = u_ref[:, MLA_KV_LORA + HEAD_W:MLA_KV_LORA + HEAD_W + MLA_Q_LORA]
    cos = cos_ref[...]
    sin = sin_ref[...]
    cos_h = jnp.concatenate([cos] * MLA_HEADS, axis=1)
    sin_h = jnp.concatenate([sin] * MLA_HEADS, axis=1)
    cqn = _rms_rows(cq, gq_ref[...]).astype(BF16)
    q_ref[...] = (_nn(cqn, wqa_ref[...]) * cos_h + _nn(cqn, wqb_ref[...]) * sin_h).astype(BF16)
    c = _rms_rows(ckv, gkv_ref[...])
    lane = lax.broadcasted_iota(jnp.int32, krb.shape, 1)
    swapped = jnp.where(lane < ROPE_MID, pltpu.roll(krb, HEAD_W - MLA_ROPE // 2, 1), pltpu.roll(krb, MLA_ROPE // 2, 1))
    rot = krb * cos + swapped * sin
    lat_ref[:, 0:MLA_KV_LORA] = c
    lat_ref[:, MLA_KV_LORA:MLA_LAT] = rot[:, ROPE_LO:ROPE_HI]
    cb = c.astype(BF16)
    k_ref[...] = (_nn(cb, wk_ref[...]) + jnp.concatenate([rot] * MLA_HEADS, axis=1)).astype(BF16)
    v_ref[...] = _nn(cb, wv_ref[...]).astype(BF16)


def _odd_prep(u_a, cos_t, sin_t, gq, gkv, wqa, wqb, wk, wv, *, tm=256):
    n = u_a.shape[0]
    row = lambda w_: pl.BlockSpec((tm, w_), lambda i: (i, 0))
    const = lambda a: pl.BlockSpec(a.shape, lambda i: (0, 0))
    hw = MLA_HEADS * HEAD_W
    return pl.pallas_call(
        _odd_prep_kernel,
        grid=(n // tm,),
        in_specs=[row(u_a.shape[1]), row(HEAD_W), row(HEAD_W), const(gq), const(gkv),
                  const(wqa), const(wqb), const(wk), const(wv)],
        out_specs=[row(MLA_LAT), row(hw), row(hw), row(MLA_HEADS * MLA_VDIM)],
        out_shape=[jax.ShapeDtypeStruct((n, MLA_LAT), F32), jax.ShapeDtypeStruct((n, hw), BF16),
                   jax.ShapeDtypeStruct((n, hw), BF16), jax.ShapeDtypeStruct((n, MLA_HEADS * MLA_VDIM), BF16)],
        compiler_params=_params(("parallel",), 40),
        name="odd_prep",
    )(u_a, cos_t, sin_t, gq, gkv, wqa, wqb, wk, wv)


def _odd_weights(w_in, w_uq, w_uk, w_uv):
    o = 0
    cq_w = w_in[:, o:o + MLA_Q_LORA]; o += MLA_Q_LORA
    ckv_w = w_in[:, o:o + MLA_KV_LORA]; o += MLA_KV_LORA
    kr_w = w_in[:, o:o + MLA_ROPE]; o += MLA_ROPE
    rest = w_in[:, o:]
    zeros = lambda w_: jnp.zeros((D_MODEL, w_), w_in.dtype)
    w_in2 = jnp.concatenate([ckv_w, zeros(ROPE_LO), kr_w, zeros(HEAD_W - ROPE_HI), cq_w, rest], axis=1)

    half = MLA_ROPE // 2
    nope = w_uq[:, :, :MLA_NOPE]
    x1 = w_uq[:, :, MLA_NOPE:MLA_NOPE + half]
    x2 = w_uq[:, :, MLA_NOPE + half:]
    pad = jnp.zeros((MLA_Q_LORA, MLA_HEADS, HEAD_W - ROPE_HI), w_uq.dtype)
    wqa = jnp.concatenate([nope, x1, x2, pad], axis=2).reshape(MLA_Q_LORA, MLA_HEADS * HEAD_W)
    wqb = jnp.concatenate([jnp.zeros_like(nope), x2, x1, pad], axis=2).reshape(MLA_Q_LORA, MLA_HEADS * HEAD_W)
    kpad = jnp.zeros((MLA_KV_LORA, MLA_HEADS, HEAD_W - MLA_NOPE), w_uk.dtype)
    wk = jnp.concatenate([w_uk, kpad], axis=2).reshape(MLA_KV_LORA, MLA_HEADS * HEAD_W)
    wv = w_uv.reshape(MLA_KV_LORA, MLA_HEADS * MLA_VDIM)
    up = jnp.zeros((MLA_HEADS, HEAD_W, MLA_LAT), F32)
    up = up.at[:, :MLA_NOPE, :MLA_KV_LORA].set(jnp.transpose(w_uk, (1, 2, 0)))
    up = up.at[:, ROPE_LO:ROPE_HI, MLA_KV_LORA:].set(jnp.broadcast_to(jnp.eye(MLA_ROPE, dtype=F32), (MLA_HEADS, MLA_ROPE, MLA_ROPE)))
    wvh = jnp.transpose(w_uv, (1, 0, 2))
    return (w_in2.astype(BF16), wqa.astype(BF16), wqb.astype(BF16), wk.astype(BF16), wv.astype(BF16),
            up.astype(BF16), wvh.astype(BF16))


def _rope_tables(pos):
    half = MLA_ROPE // 2
    inv = ROPE_THETA ** (-jnp.arange(half, dtype=F32) / half)
    ang = pos.astype(F32)[:, None] * inv[None, :]
    cos, sin = jnp.cos(ang), jnp.sin(ang)
    n = pos.shape[0]
    cos_t = jnp.concatenate([jnp.ones((n, ROPE_LO), F32), cos, cos, jnp.zeros((n, HEAD_W - ROPE_HI), F32)], axis=1)
    sin_t = jnp.concatenate([jnp.zeros((n, ROPE_LO), F32), -sin, sin, jnp.zeros((n, HEAD_W - ROPE_HI), F32)], axis=1)
    return cos_t, sin_t


BIG_NEG = -1e30


def _causal_tiles(nq):
    qs = [i for i in range(nq) for _ in range(i + 1)]
    ks = [j for i in range(nq) for j in range(i + 1)]
    return jnp.asarray(qs, jnp.int32), jnp.asarray(ks, jnp.int32)


def _mla_flash_kernel(qt_ref, kt_ref, q_ref, k_ref, v_ref, o_ref, m_scr, l_scr, acc_scr):
    step = pl.program_id(2)
    qi = qt_ref[step]
    ki = kt_ref[step]
    tq, tk = q_ref.shape[0], k_ref.shape[0]
    scale = (MLA_NOPE + MLA_ROPE) ** -0.5

    @pl.when(ki == 0)
    def _():
        m_scr[...] = jnp.full(m_scr.shape, BIG_NEG, F32)
        l_scr[...] = jnp.zeros(l_scr.shape, F32)
        acc_scr[...] = jnp.zeros(acc_scr.shape, F32)

    row = lax.broadcasted_iota(jnp.int32, (tq, tk), 0)
    col = lax.broadcasted_iota(jnp.int32, (tq, tk), 1)
    keep = col <= row + jnp.where(ki < qi, tk, 0)
    v = v_ref[...]
    for hh in range(2):
        cs = slice(HEAD_W * hh, HEAD_W * (hh + 1))
        s = jnp.where(keep, _nt(q_ref[:, cs], k_ref[:, cs]) * scale, BIG_NEG)
        m_old = m_scr[hh]
        m_new = jnp.maximum(m_old, jnp.max(s, axis=-1, keepdims=True))
        alpha = jnp.exp(m_old - m_new)
        p = jnp.exp(s - m_new[:, 0:1])
        l_scr[hh] = alpha * l_scr[hh] + jnp.sum(p, axis=-1, keepdims=True)
        acc_scr[hh] = alpha * acc_scr[hh] + _nn(p.astype(BF16), v)
        m_scr[hh] = m_new

    @pl.when(ki == qi)
    def _():
        lane = lax.broadcasted_iota(jnp.int32, (tq, 2 * MLA_VDIM), 1)
        o = jnp.where(lane < MLA_VDIM, acc_scr[0] / l_scr[0], acc_scr[1] / l_scr[1])
        o_ref[...] = o.astype(o_ref.dtype)


def _mla_prompt(q, k, v, *, n_prompt, seq, tq=512):
    nbatch = n_prompt // seq
    nq = seq // tq
    npair = MLA_HEADS // 2
    qt, kt = _causal_tiles(nq)
    grid_spec = pltpu.PrefetchScalarGridSpec(
        num_scalar_prefetch=2,
        grid=(nbatch, npair, qt.shape[0]),
        in_specs=[pl.BlockSpec((tq, 2 * HEAD_W), lambda b, h, s, qt, kt: (b * nq + qt[s], h)),
                  pl.BlockSpec((tq, 2 * HEAD_W), lambda b, h, s, qt, kt: (b * nq + kt[s], h)),
                  pl.BlockSpec((tq, 2 * MLA_VDIM), lambda b, h, s, qt, kt: (b * nq + kt[s], h))],
        out_specs=pl.BlockSpec((tq, 2 * MLA_VDIM), lambda b, h, s, qt, kt: (b * nq + qt[s], h)),
        scratch_shapes=[pltpu.VMEM((2, tq, LANES), F32), pltpu.VMEM((2, tq, LANES), F32),
                        pltpu.VMEM((2, tq, 2 * MLA_VDIM), F32)],
    )
    return pl.pallas_call(
        _mla_flash_kernel,
        grid_spec=grid_spec,
        out_shape=jax.ShapeDtypeStruct((n_prompt, MLA_HEADS * MLA_VDIM), BF16),
        compiler_params=_params(("parallel", "parallel", "arbitrary"), 32),
        name="mla_prompt_attention",
    )(qt, kt, q, k, v)


def _absorb_kernel(q_ref, up_ref, o_ref):
    for h in range(MLA_HEADS):
        o_ref[h] = _nn(q_ref[:, HEAD_W * h:HEAD_W * (h + 1)], up_ref[h])


def _absorb(q, up, *, row0, nrows, tm=128):
    assert row0 % tm == 0 and nrows % tm == 0
    return pl.pallas_call(
        _absorb_kernel,
        grid=(nrows // tm,),
        in_specs=[pl.BlockSpec((tm, MLA_HEADS * HEAD_W), lambda i: (row0 // tm + i, 0)),
                  pl.BlockSpec(up.shape, lambda i: (0, 0, 0))],
        out_specs=pl.BlockSpec((MLA_HEADS, tm, MLA_LAT), lambda i: (0, i, 0)),
        out_shape=jax.ShapeDtypeStruct((MLA_HEADS, nrows, MLA_LAT), F32),
        compiler_params=_params(("parallel",), 32),
        name="mla_absorb_query",
    )(q, up)


def _page_copy(cache_hbm, buf, sem, page, slot, j):
    return pltpu.make_async_copy(cache_hbm.at[page], buf.at[slot, :, pl.ds(j * PAGE_SIZE, PAGE_SIZE)],
                                 sem.at[slot])


def _mla_sample_kernel(pt_ref, q_ref, new_ref, cache_hbm, o_ref, buf, sem, *, pg, nchunk):
    b = pl.program_id(0)
    t_new = new_ref.shape[0]
    nrow = MLA_HEADS * t_new
    scale = (MLA_NOPE + MLA_ROPE) ** -0.5

    def start(chunk, slot):
        for j in range(pg):
            _page_copy(cache_hbm, buf, sem, pt_ref[b, chunk * pg + j], slot, j).start()

    def wait(slot):
        for j in range(pg):
            _page_copy(cache_hbm, buf, sem, 0, slot, j).wait()

    start(0, 0)
    q = q_ref[...].reshape(nrow, MLA_LAT).astype(BF16)

    def body(c, carry):
        m, l, acc = carry
        slot = c % 2

        @pl.when(c + 1 < nchunk)
        def _():
            start(c + 1, 1 - slot)

        wait(slot)
        kc = buf[slot].astype(BF16)
        s = _nn(q, kc) * scale
        m_new = jnp.maximum(m, jnp.max(s, axis=-1, keepdims=True))
        alpha = jnp.exp(m - m_new)
        p = jnp.exp(s - m_new)
        l = alpha * l + jnp.sum(p, axis=-1, keepdims=True)
        acc = alpha * acc + _nt(p.astype(BF16), kc[0:MLA_KV_LORA, :])
        return m_new, l, acc

    init = (jnp.full((nrow, 1), BIG_NEG, F32), jnp.zeros((nrow, 1), F32), jnp.zeros((nrow, MLA_KV_LORA), F32))
    m, l, acc = lax.fori_loop(0, nchunk, body, init)

    kn = new_ref[...].astype(BF16)
    tq = lax.broadcasted_iota(jnp.int32, (nrow, t_new), 0) % t_new
    tk = lax.broadcasted_iota(jnp.int32, (nrow, t_new), 1)
    s = jnp.where(tk <= tq, _nt(q, kn) * scale, BIG_NEG)
    m_new = jnp.maximum(m, jnp.max(s, axis=-1, keepdims=True))
    alpha = jnp.exp(m - m_new)
    p = jnp.exp(s - m_new)
    l = alpha * l + jnp.sum(p, axis=-1, keepdims=True)
    acc = alpha * acc + _nn(p.astype(BF16), kn[:, 0:MLA_KV_LORA])
    o_ref[...] = (acc / l).reshape(MLA_HEADS, t_new, MLA_KV_LORA)


def _mla_sample(q_abs, lat, cache, page_table, *, row0, t_new, pg=16):
    nb, npages = page_table.shape
    assert npages % pg == 0 and t_new == SUBLANES and row0 % t_new == 0
    cache = jnp.transpose(cache, (0, 2, 1))
    grid_spec = pltpu.PrefetchScalarGridSpec(
        num_scalar_prefetch=1,
        grid=(nb,),
        in_specs=[pl.BlockSpec((MLA_HEADS, t_new, MLA_LAT), lambda b, pt: (0, b, 0)),
                  pl.BlockSpec((t_new, MLA_LAT), lambda b, pt: (row0 // t_new + b, 0)),
                  pl.BlockSpec(memory_space=pl.ANY)],
        out_specs=pl.BlockSpec((MLA_HEADS, t_new, MLA_KV_LORA), lambda b, pt: (0, b, 0)),
        scratch_shapes=[pltpu.VMEM((2, MLA_LAT, pg * PAGE_SIZE), F32), pltpu.SemaphoreType.DMA((2,))],
    )
    return pl.pallas_call(
        functools.partial(_mla_sample_kernel, pg=pg, nchunk=npages // pg),
        grid_spec=grid_spec,
        out_shape=jax.ShapeDtypeStruct((MLA_HEADS, nb * t_new, MLA_KV_LORA), F32),
        compiler_params=_params(("arbitrary",), 40),
        name="mla_sample_attention",
    )(page_table, q_abs, lat, cache)


def _latent_up_kernel(o_ref, w_ref, y_ref):
    outs = [_nn(o_ref[h].astype(BF16), w_ref[h]) for h in range(MLA_HEADS)]
    y_ref[...] = jnp.concatenate(outs, axis=1).astype(y_ref.dtype)


def _latent_up(o_lat, wvh, *, tm=128):
    nrows = o_lat.shape[1]
    return pl.pallas_call(
        _latent_up_kernel,
        grid=(nrows // tm,),
        in_specs=[pl.BlockSpec((MLA_HEADS, tm, MLA_KV_LORA), lambda i: (0, i, 0)),
                  pl.BlockSpec(wvh.shape, lambda i: (0, 0, 0))],
        out_specs=pl.BlockSpec((tm, MLA_HEADS * MLA_VDIM), lambda i: (i, 0)),
        out_shape=jax.ShapeDtypeStruct((nrows, MLA_HEADS * MLA_VDIM), BF16),
        compiler_params=_params(("parallel",), 32),
        name="mla_latent_up",
    )(o_lat, wvh)


def _block_mean_kernel(k_ref, o_ref, *, nblk):
    k = k_ref[...].reshape(nblk, MOBA_BLOCK, D_KV)
    o_ref[0, 0:nblk, :] = jnp.mean(k, axis=1)
    if nblk < o_ref.shape[1]:
        o_ref[0, nblk:, :] = jnp.zeros((o_ref.shape[1] - nblk, D_KV), F32)


def _block_means(u_b, *, n_prompt, seq):
    nbatch = n_prompt // seq
    nblk = seq // MOBA_BLOCK
    assert seq % MOBA_BLOCK == 0 and nblk <= LANES
    return pl.pallas_call(
        functools.partial(_block_mean_kernel, nblk=nblk),
        grid=(nbatch,),
        in_specs=[pl.BlockSpec((seq, D_KV), lambda b: (b, D_Q // D_KV))],
        out_specs=pl.BlockSpec((1, LANES, D_KV), lambda b: (b, 0, 0)),
        out_shape=jax.ShapeDtypeStruct((nbatch, LANES, D_KV), F32),
        compiler_params=_params(("parallel",), 32),
        name="moba_block_means",
    )(u_b)


def _top_blocks(gate, limit):
    lane = lax.broadcasted_iota(jnp.int32, gate.shape, 1)
    g = jnp.where(lane < limit, gate, NEG_INF)
    sel = jnp.zeros(gate.shape, F32)
    for j in range(MOBA_TOPK):
        mx = jnp.max(g, axis=-1, keepdims=True)
        idx = jnp.min(jnp.where(g == mx, lane, LANES), axis=-1, keepdims=True)
        hit = lane == idx + jnp.where(j < limit, 0, 2 * LANES)
        sel = jnp.where(hit, 1.0, sel)
        g = jnp.where(lane == idx, NEG_INF, g)
    return sel


def _moba_flash_kernel(qt_ref, kt_ref, q_ref, k_ref, v_ref, km_ref, o_ref, sel_scr, m_scr, l_scr, acc_scr):
    step = pl.program_id(2)
    qi = qt_ref[step]
    ki = kt_ref[step]
    tq, tk = q_ref.shape[0], k_ref.shape[0]
    scale = MOBA_HDIM ** -0.5
    grp = MOBA_HEADS // MOBA_KV_HEADS
    nh = 2 * grp

    @pl.when(ki == 0)
    def _():
        m_scr[...] = jnp.full(m_scr.shape, BIG_NEG, F32)
        l_scr[...] = jnp.zeros(l_scr.shape, F32)
        acc_scr[...] = jnp.zeros(acc_scr.shape, F32)
        for hh in range(nh):
            kv = hh // grp
            qh = q_ref[:, MOBA_HDIM * hh:MOBA_HDIM * (hh + 1)].astype(BF16)
            km = km_ref[0, :, MOBA_HDIM * kv:MOBA_HDIM * (kv + 1)].astype(BF16)
            sel_scr[hh] = _top_blocks(_nt(qh, km), qi)

    row = lax.broadcasted_iota(jnp.int32, (tq, tk), 0)
    col = lax.broadcasted_iota(jnp.int32, (tq, tk), 1)
    causal = col <= row + jnp.where(ki < qi, tk, 0)
    lane = lax.broadcasted_iota(jnp.int32, (tq, LANES), 1)
    own = jnp.where(ki == qi, 1.0, 0.0)
    for hh in range(nh):
        kv = hh // grp
        ks = slice(MOBA_HDIM * kv, MOBA_HDIM * (kv + 1))
        qh = q_ref[:, MOBA_HDIM * hh:MOBA_HDIM * (hh + 1)].astype(BF16)
        picked = jnp.sum(jnp.where(lane == ki, sel_scr[hh], 0.0), axis=-1, keepdims=True) + own
        keep = causal & (picked > 0.5)
        s = jnp.where(keep, _nt(qh, k_ref[:, ks].astype(BF16)) * scale, BIG_NEG)
        m_old = m_scr[hh]
        m_new = jnp.maximum(m_old, jnp.max(s, axis=-1, keepdims=True))
        alpha = jnp.exp(m_old - m_new)
        p = jnp.where(keep, jnp.exp(s - m_new[:, 0:1]), 0.0)
        l_scr[hh] = alpha * l_scr[hh] + jnp.sum(p, axis=-1, keepdims=True)
        acc_scr[hh] = alpha[:, 0:MOBA_HDIM] * acc_scr[hh] + _nn(p.astype(BF16), v_ref[:, ks].astype(BF16))
        m_scr[hh] = m_new

    @pl.when(ki == qi)
    def _():
        o_ref[...] = jnp.concatenate([acc_scr[hh] / l_scr[hh][:, 0:MOBA_HDIM] for hh in range(nh)],
                                     axis=1).astype(o_ref.dtype)


def _moba_prompt(u_b, kmean, *, n_prompt, seq):
    nbatch = n_prompt // seq
    tq = MOBA_BLOCK
    nq = seq // tq
    npair = MOBA_KV_HEADS // 2
    grp = MOBA_HEADS // MOBA_KV_HEADS
    qw = 2 * grp * MOBA_HDIM
    kw = 2 * MOBA_HDIM
    kcb = D_Q // kw
    vcb = (D_Q + D_KV) // kw
    qt, kt = _causal_tiles(nq)
    grid_spec = pltpu.PrefetchScalarGridSpec(
        num_scalar_prefetch=2,
        grid=(nbatch, npair, qt.shape[0]),
        in_specs=[pl.BlockSpec((tq, qw), lambda b, h, s, qt, kt: (b * nq + qt[s], h)),
                  pl.BlockSpec((tq, kw), lambda b, h, s, qt, kt: (b * nq + kt[s], kcb + h)),
                  pl.BlockSpec((tq, kw), lambda b, h, s, qt, kt: (b * nq + kt[s], vcb + h)),
                  pl.BlockSpec((1, LANES, kw), lambda b, h, s, qt, kt: (b, 0, h))],
        out_specs=pl.BlockSpec((tq, qw), lambda b, h, s, qt, kt: (b * nq + qt[s], h)),
        scratch_shapes=[pltpu.VMEM((2 * grp, tq, LANES), F32), pltpu.VMEM((2 * grp, tq, LANES), F32),
                        pltpu.VMEM((2 * grp, tq, LANES), F32), pltpu.VMEM((2 * grp, tq, MOBA_HDIM), F32)],
    )
    return pl.pallas_call(
        _moba_flash_kernel,
        grid_spec=grid_spec,
        out_shape=jax.ShapeDtypeStruct((n_prompt, D_Q), BF16),
        compiler_params=_params(("parallel", "parallel", "arbitrary"), 32),
        name="moba_prompt_attention",
    )(qt, kt, u_b, u_b, u_b, kmean)


def _moba_sample_kernel(pt_ref, q_ref, kvn_ref, cache_hbm, o_ref, buf, sem, km_scr, m_scr, l_scr, o_scr,
                        *, pg, nchunk):
    b = pl.program_id(0)
    t_new = q_ref.shape[0]
    grp = MOBA_HEADS // MOBA_KV_HEADS
    nrow = MOBA_HEADS * t_new
    scale = MOBA_HDIM ** -0.5
    bpc = pg * PAGE_SIZE // MOBA_BLOCK
    nblk = nchunk * bpc

    def start(chunk, slot):
        for j in range(pg):
            _page_copy(cache_hbm, buf, sem, pt_ref[b, chunk * pg + j], slot, j).start()

    def wait(slot):
        for j in range(pg):
            _page_copy(cache_hbm, buf, sem, 0, slot, j).wait()

    start(0, 0)
    q = q_ref[...]
    pieces = []
    for h in range(MOBA_HEADS):
        kv = h // grp
        parts = []
        if kv > 0:
            parts.append(jnp.zeros((t_new, MOBA_HDIM * kv), F32))
        parts.append(q[:, MOBA_HDIM * h:MOBA_HDIM * (h + 1)])
        if kv < MOBA_KV_HEADS - 1:
            parts.append(jnp.zeros((t_new, MOBA_HDIM * (MOBA_KV_HEADS - 1 - kv)), F32))
        pieces.append(jnp.concatenate(parts, axis=1))
    qbd = jnp.concatenate(pieces, axis=0).astype(BF16)
    km_scr[...] = jnp.zeros(km_scr.shape, F32)
    m_scr[...] = jnp.full(m_scr.shape, BIG_NEG, F32)
    l_scr[...] = jnp.zeros(l_scr.shape, F32)
    lane_k = lax.broadcasted_iota(jnp.int32, (D_KV, LANES), 1)
    lane_r = lax.broadcasted_iota(jnp.int32, (nrow, LANES), 1)

    def body(c, carry):
        slot = c % 2

        @pl.when(c + 1 < nchunk)
        def _():
            start(c + 1, 1 - slot)

        wait(slot)
        s = _nn(qbd, buf[slot, 0:D_KV, :].astype(BF16)) * scale
        for j in range(bpc):
            n = c * bpc + j
            cols = slice(j * MOBA_BLOCK, (j + 1) * MOBA_BLOCK)
            kmean = jnp.mean(buf[slot, 0:D_KV, cols], axis=1, keepdims=True)
            km_scr[...] = jnp.where(lane_k == n, kmean, km_scr[...])
            sj = s[:, cols]
            m = jnp.max(sj, axis=-1, keepdims=True)
            p = jnp.exp(sj - m)
            m_scr[...] = jnp.where(lane_r == n, m, m_scr[...])
            l_scr[...] = jnp.where(lane_r == n, jnp.sum(p, axis=-1, keepdims=True), l_scr[...])
            o_scr[n] = _nt(p.astype(BF16), buf[slot, D_KV:2 * D_KV, cols].astype(BF16))
        return carry

    lax.fori_loop(0, nchunk, body, 0)

    kn = kvn_ref[:, 0:D_KV].astype(BF16)
    vn = kvn_ref[:, D_KV:2 * D_KV].astype(BF16)
    tq = lax.broadcasted_iota(jnp.int32, (nrow, t_new), 0) % t_new
    tk = lax.broadcasted_iota(jnp.int32, (nrow, t_new), 1)
    s_own = jnp.where(tk <= tq, _nt(qbd, kn) * scale, BIG_NEG)
    m_own = jnp.max(s_own, axis=-1, keepdims=True)
    p_own = jnp.exp(s_own - m_own)
    l_own = jnp.sum(p_own, axis=-1, keepdims=True)
    o_own = _nn(p_own.astype(BF16), vn)

    picked = _top_blocks(_nn(qbd, km_scr[...].astype(BF16)), nblk) > 0.5
    m_all = m_scr[...]
    mx = jnp.maximum(jnp.max(jnp.where(picked, m_all, BIG_NEG), axis=-1, keepdims=True), m_own)
    w_all = jnp.where(picked, jnp.exp(m_all - mx), 0.0)
    w_own = jnp.exp(m_own - mx)
    l = jnp.sum(w_all * l_scr[...], axis=-1, keepdims=True) + w_own * l_own
    o = w_own * o_own
    for n in range(nblk):
        o = o + w_all[:, n:n + 1] * o_scr[n]
    o = o / l
    outs = []
    for h in range(MOBA_HEADS):
        kv = h // grp
        outs.append(o[h * t_new:(h + 1) * t_new, MOBA_HDIM * kv:MOBA_HDIM * (kv + 1)])
    o_ref[...] = jnp.concatenate(outs, axis=1)


def _moba_sample(u_b, cache, page_table, *, row0, t_new, pg=8):
    nb, npages = page_table.shape
    past = npages * PAGE_SIZE
    assert npages % pg == 0 and past % MOBA_BLOCK == 0 and (pg * PAGE_SIZE) % MOBA_BLOCK == 0
    nblk = past // MOBA_BLOCK
    assert nblk <= LANES and t_new == SUBLANES
    nrow = MOBA_HEADS * t_new
    cachev = jnp.transpose(cache, (0, 2, 3, 4, 1)).reshape(cache.shape[0], 2 * D_KV, PAGE_SIZE)
    rb0 = row0 // t_new
    grid_spec = pltpu.PrefetchScalarGridSpec(
        num_scalar_prefetch=1,
        grid=(nb,),
        in_specs=[pl.BlockSpec((t_new, D_Q), lambda b, pt: (rb0 + b, 0)),
                  pl.BlockSpec((t_new, 2 * D_KV), lambda b, pt: (rb0 + b, D_Q // (2 * D_KV))),
                  pl.BlockSpec(memory_space=pl.ANY)],
        out_specs=pl.BlockSpec((t_new, D_Q), lambda b, pt: (b, 0)),
        scratch_shapes=[pltpu.VMEM((2, 2 * D_KV, pg * PAGE_SIZE), F32), pltpu.SemaphoreType.DMA((2,)),
                        pltpu.VMEM((D_KV, LANES), F32),
                        pltpu.VMEM((nrow, LANES), F32), pltpu.VMEM((nrow, LANES), F32),
                        pltpu.VMEM((nblk, nrow, D_KV), F32)],
    )
    return pl.pallas_call(
        functools.partial(_moba_sample_kernel, pg=pg, nchunk=npages // pg),
        grid_spec=grid_spec,
        out_shape=jax.ShapeDtypeStruct((nb * t_new, D_Q), F32),
        compiler_params=_params(("arbitrary",), 40),
        name="moba_sample_attention",
    )(page_table, u_b, u_b, cachev)


def _odd_out_kernel(ocp_ref, ocs_ref, odp_ref, ods_ref, x_ref, w_ref, g_ref, b_ref, y_ref, *, npb):
    half = MLA_HEADS * MLA_VDIM

    def compute(oc_ref, od_ref):
        m = (_nn(oc_ref[...].astype(BF16), w_ref[0:half, :])
             + _nn(od_ref[...].astype(BF16), w_ref[half:half + D_Q, :]))
        y_ref[...] = _layer_norm_rows(DN_ALPHA * x_ref[...] + m, g_ref[...], b_ref[...])

    i = pl.program_id(0)

    @pl.when(i < npb)
    def _():
        compute(ocp_ref, odp_ref)

    @pl.when(i >= npb)
    def _():
        compute(ocs_ref, ods_ref)


def _odd_out(oc, od, x, w_bf16, g, b, *, n_prompt, tm=512):
    n = x.shape[0]
    assert n % tm == 0 and n_prompt % tm == 0
    npb = n_prompt // tm
    const = lambda a: pl.BlockSpec(a.shape, lambda i: (0, 0))
    in_specs = (_two_source_specs(oc[0].shape[1], tm, npb) + _two_source_specs(od[0].shape[1], tm, npb)
                + [pl.BlockSpec((tm, D_MODEL), lambda i: (i, 0)), const(w_bf16), const(g), const(b)])
    return pl.pallas_call(
        functools.partial(_odd_out_kernel, npb=npb),
        grid=(n // tm,),
        in_specs=in_specs,
        out_specs=pl.BlockSpec((tm, D_MODEL), lambda i: (i, 0)),
        out_shape=jax.ShapeDtypeStruct((n, D_MODEL), F32),
        compiler_params=_params(("arbitrary",), 40),
        name="odd_out_deepnorm",
    )(*oc, *od, x, w_bf16, g, b)


def _odd_layer(x, cache_mla, cache_moba, page_table, w_in, gq, w_uq, gkv, w_uk, w_uv, w_out, ln_g, ln_b,
               *, n_prompt, seq, t_new):
    n_tok = x.shape[0]
    n_sample = n_tok - n_prompt
    past = page_table.shape[1] * PAGE_SIZE
    w_in2, wqa, wqb, wk, wv, up, wvh = _odd_weights(w_in, w_uq, w_uk, w_uv)
    wa = MLA_KV_LORA + HEAD_W + MLA_Q_LORA
    u_a, u_b = _matmul(x, w_in2, splits=(wa, D_Q + 2 * D_KV))
    pos = jnp.concatenate([jnp.tile(jnp.arange(seq, dtype=jnp.int32), n_prompt // seq),
                           jnp.tile(past + jnp.arange(t_new, dtype=jnp.int32), n_sample // t_new)])
    cos_t, sin_t = _rope_tables(pos)
    lat, q, k, v = _odd_prep(u_a, cos_t, sin_t, gq.reshape(1, -1), gkv.reshape(1, -1), wqa, wqb, wk, wv)
    oc_p = _mla_prompt(q, k, v, n_prompt=n_prompt, seq=seq)
    q_abs = _absorb(q, up, row0=n_prompt, nrows=n_sample)
    o_lat = _mla_sample(q_abs, lat, cache_mla, page_table, row0=n_prompt, t_new=t_new)
    oc_s = _latent_up(o_lat, wvh)
    kmean = _block_means(u_b, n_prompt=n_prompt, seq=seq)
    od_p = _moba_prompt(u_b, kmean, n_prompt=n_prompt, seq=seq)
    od_s = _moba_sample(u_b, cache_moba, page_table, row0=n_prompt, t_new=t_new)
    w_out_b = w_out.astype(BF16)
    g2, b2 = ln_g.reshape(1, -1), ln_b.reshape(1, -1)
    y = _odd_out((oc_p, oc_s), (od_p, od_s), x, w_out_b, g2, b2, n_prompt=n_prompt)
    kv_new = u_b[:, D_Q:]
    return y, lat, kv_new


def kernel(x_prompt, x_sample, state_hgrn, cache_swa_w128, cache_swa_w512, cache_swa_w2048, cache_mla, cache_moba_kv, page_table, w_in_even, hgrn_lb_logits, hgrn_norm_g, w_out_even, w_in_odd, mla_q_norm_g, mla_w_uq, mla_kv_norm_g, mla_w_uk, mla_w_uv, w_out_odd, ln_mix_g, ln_mix_b, ln_ffn_g, ln_ffn_b, router_group_w, router_group_b, router_expert_w, router_expert_b, expert_w_gate, expert_w_up, expert_w_down):
    nbatch, seq, _ = x_prompt.shape
    nb_s, t_new, _ = x_sample.shape
    n_prompt = nbatch * seq
    x = jnp.concatenate([x_prompt.reshape(n_prompt, D_MODEL), x_sample.reshape(nb_s * t_new, D_MODEL)], axis=0)
    caches = [cache_swa_w128, cache_swa_w512, cache_swa_w2048]
    x, hgrn_p, hgrn_s, swa_p, swa_s = _even_layer(
        x, state_hgrn, caches, w_in_even, hgrn_lb_logits, hgrn_norm_g, w_out_even, ln_mix_g[0], ln_mix_b[0],
        n_prompt=n_prompt, seq=seq, t_new=t_new)

    def moe(x, layer):
        wr, br = _router_weights(router_group_w[layer], router_group_b[layer],
                                 router_expert_w[layer], router_expert_b[layer])
        return _hier_moe(x, wr, br, expert_w_gate[layer].astype(BF16), expert_w_up[layer].astype(BF16),
                         expert_w_down[layer].astype(BF16),
                         ln_ffn_g[layer].reshape(1, -1), ln_ffn_b[layer].reshape(1, -1))

    x = moe(x, 0)
    x, lat, kv_new = _odd_layer(
        x, cache_mla, cache_moba_kv, page_table, w_in_odd, mla_q_norm_g, mla_w_uq, mla_kv_norm_g, mla_w_uk,
        mla_w_uv, w_out_odd, ln_mix_g[1], ln_mix_b[1], n_prompt=n_prompt, seq=seq, t_new=t_new)
    x = moe(x, 1)
    kv_shape = (2, MOBA_KV_HEADS, MOBA_HDIM)
    return (x[:n_prompt].reshape(x_prompt.shape), x[n_prompt:].reshape(x_sample.shape),
            hgrn_p, hgrn_s, swa_p[0], swa_s[0], swa_p[1], swa_s[1], swa_p[2], swa_s[2],
            lat[:n_prompt].reshape(nbatch, seq, MLA_LAT), lat[n_prompt:].reshape(nb_s, t_new, MLA_LAT),
            kv_new[:n_prompt].reshape((nbatch, seq) + kv_shape), kv_new[n_prompt:].reshape((nb_s, t_new) + kv_shape))
```

```python
import functools
import math

import jax
import jax.numpy as jnp
from jax import lax
from jax.experimental import pallas as pl
from jax.experimental.pallas import tpu as pltpu

F32 = jnp.float32
BF16 = jnp.bfloat16

D_MODEL = 1024
DEPTH = 2
PAGE_SIZE = 128
HGRN_HEADS = 4
HGRN_KDIM = 128
HGRN_VDIM = 128
HGRN_CHUNK = 32
DIL_GROUPS = ((128, 1), (512, 4), (2048, 16))
DIL_HEADS = 4
DIL_HDIM = 64
MLA_HEADS = 8
MLA_Q_LORA = 384
MLA_KV_LORA = 256
MLA_NOPE = 64
MLA_ROPE = 32
MLA_VDIM = 64
ROPE_THETA = 10000.0
MOBA_HEADS = 8
MOBA_KV_HEADS = 4
MOBA_HDIM = 64
MOBA_BLOCK = 256
MOBA_TOPK = 3
MOE_GROUPS = 4
MOE_EXPERTS_PER_GROUP = 8
MOE_EXPERTS = MOE_GROUPS * MOE_EXPERTS_PER_GROUP
MOE_TOPK = 2
MOE_FF = 512
DN_ALPHA = (2 * DEPTH) ** 0.25
LN_EPS = 1e-5
RMS_EPS = 1e-6

A_QF = HGRN_HEADS * HGRN_KDIM
A_IV = HGRN_HEADS * HGRN_VDIM
B_W = DIL_HEADS * DIL_HDIM
EVEN_IN = 2 * A_QF + 2 * A_IV + 3 * len(DIL_GROUPS) * B_W
MLA_LAT = MLA_KV_LORA + MLA_ROPE
D_Q = MOBA_HEADS * MOBA_HDIM
D_KV = MOBA_KV_HEADS * MOBA_HDIM

LANES = 128
SUBLANES = 8
MIB = 2 ** 20

NEG_INF = float("-inf")


def _params(semantics, vmem_mib):
    return pltpu.CompilerParams(dimension_semantics=semantics, vmem_limit_bytes=vmem_mib * MIB)


def _nt(a, b):
    return lax.dot_general(a, b, (((1,), (1,)), ((), ())), preferred_element_type=F32)


def _tn(a, b):
    return lax.dot_general(a, b, (((0,), (0,)), ((), ())), preferred_element_type=F32)


def _nn(a, b):
    return jnp.dot(a, b, preferred_element_type=F32)


def _layer_norm_rows(z, g, b):
    zc = z - jnp.mean(z, axis=-1, keepdims=True)
    y = zc * lax.rsqrt(jnp.mean(zc * zc, axis=-1, keepdims=True) + LN_EPS)
    return y * g + b


def _matmul_kernel(x_ref, w_ref, *refs, tn, nsplit, residue_views):
    o_refs = refs[:nsplit]
    xb = x_ref[...].astype(BF16)
    base = 0
    for o_ref in o_refs:
        n = o_ref.shape[1]
        for c0 in range(0, n, tn):
            c1 = min(c0 + tn, n)
            o_ref[:, c0:c1] = _nn(xb, w_ref[:, base + c0:base + c1]).astype(o_ref.dtype)
        base += n
    if not residue_views:
        return
    stage = refs[-1]
    tm = x_ref.shape[0]
    for (c0, width, dil), r_ref in zip(residue_views, refs[nsplit:-1]):
        for cb in range(width // LANES):
            stage[...] = o_refs[0][:, c0 + cb * LANES:c0 + (cb + 1) * LANES]
            for r in range(dil):
                r_ref[:, r * width + cb * LANES:r * width + (cb + 1) * LANES] = (
                    stage[pl.ds(r, tm // dil, stride=dil), :])


def _matmul(x, w_bf16, *, splits=None, residue_views=(), tm=256, tn=512):
    m, k = x.shape
    n = w_bf16.shape[1]
    splits = splits or (n,)
    assert m % tm == 0 and sum(splits) == n and all(s % LANES == 0 for s in splits)
    assert all(tm % (SUBLANES * d) == 0 and wd % LANES == 0 for _, wd, d in residue_views)
    out_specs = [pl.BlockSpec((tm, s), lambda i: (i, 0)) for s in splits]
    out_shape = [jax.ShapeDtypeStruct((m, s), F32) for s in splits]
    for _, wd, d in residue_views:
        out_specs.append(pl.BlockSpec((tm // d, d * wd), lambda i: (i, 0)))
        out_shape.append(jax.ShapeDtypeStruct((m // d, d * wd), F32))
    outs = pl.pallas_call(
        functools.partial(_matmul_kernel, tn=tn, nsplit=len(splits), residue_views=tuple(residue_views)),
        grid=(m // tm,),
        in_specs=[pl.BlockSpec((tm, k), lambda i: (i, 0)),
                  pl.BlockSpec((k, n), lambda i: (0, 0))],
        out_specs=out_specs,
        out_shape=out_shape,
        scratch_shapes=[pltpu.VMEM((tm, LANES), F32)] if residue_views else [],
        compiler_params=_params(("parallel",), 48),
        name="dense_projection",
    )(x, w_bf16)
    return outs[0] if len(outs) == 1 else outs


def _chunk_cumsum(x, c):
    row = lax.broadcasted_iota(jnp.int32, x.shape, 0) & (c - 1)
    y = x
    s = 1
    while s < c:
        y = y + jnp.where(row >= s, pltpu.roll(y, s, 0), 0.0)
        s *= 2
    return y


def _hgrn_kernel(q_ref, f_ref, i_ref, g_ref, lb_ref, ng_ref, s0_ref, o_ref, sfin_ref, st_scr,
                 *, c, sb, nsb):
    t = pl.program_id(1)

    @pl.when(t == 0)
    def _():
        for h in range(HGRN_HEADS):
            st_scr[h] = s0_ref[0, h].T

    nchunk = sb // c
    row = lax.broadcasted_iota(jnp.int32, (sb, sb), 0)
    col = lax.broadcasted_iota(jnp.int32, (sb, sb), 1)
    causal = (col <= row) & ((row // c) == (col // c))
    ng = ng_ref[...]

    def sub(j, carry):
        rows = pl.ds(pl.multiple_of(j * sb, sb), sb)
        for h in range(HGRN_HEADS):
            cs = slice(HGRN_KDIM * h, HGRN_KDIM * (h + 1))
            q = q_ref[rows, cs]
            fp = f_ref[rows, cs]
            v = i_ref[rows, cs]
            g = g_ref[rows, cs]
            lb = lb_ref[:, cs]
            lf = jnp.log(lb + (1.0 - lb) * jax.nn.sigmoid(fp))
            k = (1.0 - lb) * jax.nn.sigmoid(-fp)
            cum = _chunk_cumsum(lf, c)
            last = jnp.concatenate(
                [jnp.broadcast_to(cum[ci * c + c - 1:ci * c + c, :], (c, HGRN_KDIM)) for ci in range(nchunk)],
                axis=0)
            qd = q * jnp.exp(cum)
            kd = k * jnp.exp(-cum)
            kl = k * jnp.exp(last - cum)
            att = jnp.where(causal, _nt(qd, kd), 0.0)
            intra = _nn(att, v)
            st = st_scr[h]
            outs = []
            for ci in range(nchunk):
                r = slice(ci * c, (ci + 1) * c)
                outs.append(intra[r] + _nt(qd[r], st))
                st = st * jnp.exp(last[ci * c:ci * c + 1, :]) + _tn(v[r], kl[r])
            st_scr[h] = st
            o = jnp.concatenate(outs, axis=0) if nchunk > 1 else outs[0]
            y = o * lax.rsqrt(jnp.mean(o * o, axis=-1, keepdims=True) + RMS_EPS) * ng
            o_ref[rows, cs] = (y * (g * jax.nn.sigmoid(g))).astype(o_ref.dtype)
        return carry

    lax.fori_loop(0, nsb, sub, 0)

    @pl.when(t == pl.num_programs(1) - 1)
    def _():
        for h in range(HGRN_HEADS):
            sfin_ref[0, h] = st_scr[h].T


def _hgrn(u, lb, norm_g, s0, *, row0, seq, tb, c):
    nb = s0.shape[0]
    nt = seq // tb
    sb = min(tb, LANES)
    assert seq % tb == 0 and tb % sb == 0 and sb % c == 0 and row0 % tb == 0
    rb0 = row0 // tb
    in_specs = [pl.BlockSpec((tb, A_QF), lambda b, t, j=j: (rb0 + b * nt + t, j)) for j in range(4)]
    in_specs += [pl.BlockSpec((1, A_QF), lambda b, t: (0, 0)),
                 pl.BlockSpec((1, HGRN_VDIM), lambda b, t: (0, 0)),
                 pl.BlockSpec((1, HGRN_HEADS, HGRN_KDIM, HGRN_VDIM), lambda b, t: (b, 0, 0, 0))]
    return pl.pallas_call(
        functools.partial(_hgrn_kernel, c=c, sb=sb, nsb=tb // sb),
        grid=(nb, nt),
        in_specs=in_specs,
        out_specs=[pl.BlockSpec((tb, A_IV), lambda b, t: (b * nt + t, 0)),
                   pl.BlockSpec((1, HGRN_HEADS, HGRN_KDIM, HGRN_VDIM), lambda b, t: (b, 0, 0, 0))],
        out_shape=[jax.ShapeDtypeStruct((nb * seq, A_IV), F32),
                   jax.ShapeDtypeStruct(s0.shape, F32)],
        scratch_shapes=[pltpu.VMEM((HGRN_HEADS, HGRN_VDIM, HGRN_KDIM), F32)],
        compiler_params=_params(("parallel", "arbitrary"), 48),
        name="hgrn2_scan",
    )(u, u, u, u, lb, norm_g, s0)


def _dil_prompt_kernel(q_ref, kc_ref, vc_ref, kp_ref, vp_ref, o_ref, l_ref):
    w = q_ref.shape[0]
    n = pl.program_id(1)
    a = lax.broadcasted_iota(jnp.int32, (w, 2 * w), 0)
    cc = lax.broadcasted_iota(jnp.int32, (w, 2 * w), 1)
    lo = a + jnp.where(n > 0, 0, 2 * w)
    valid = ((cc >= w) & ((cc - w) <= a)) | ((cc < w) & (cc >= lo))
    scale = DIL_HDIM ** -0.5
    outs, lses = [], []
    for h in range(DIL_HEADS):
        cs = slice(DIL_HDIM * h, DIL_HDIM * (h + 1))
        q = q_ref[:, cs].astype(BF16)
        k = jnp.concatenate([kp_ref[:, cs], kc_ref[:, cs]], axis=0).astype(BF16)
        v = jnp.concatenate([vp_ref[:, cs], vc_ref[:, cs]], axis=0).astype(BF16)
        s = jnp.where(valid, _nt(q, k) * scale, NEG_INF)
        m = jnp.max(s, axis=-1, keepdims=True)
        e = jnp.exp(s - m)
        l = jnp.sum(e, axis=-1, keepdims=True)
        outs.append(_nn(e.astype(BF16), v) / l)
        lses.append(jnp.broadcast_to(m + jnp.log(l), (w, DIL_HDIM)))
    o_ref[...] = jnp.concatenate(outs, axis=1)
    l_ref[...] = jnp.concatenate(lses, axis=1)


def _dil_prompt(uv, grp, *, n_prompt, seq, ncb, cb0):
    win, dil = DIL_GROUPS[grp]
    w = win // dil
    unit = dil * w
    assert seq % unit == 0 and w == LANES and uv.shape[1] == dil * ncb * B_W
    nbatch = n_prompt // seq
    nblk = seq // unit

    def spec(cb, prev):
        def imap(b, n, r):
            nn_ = jnp.maximum(n - 1, 0) if prev else n
            return (b * nblk + nn_, r * ncb + cb)
        return pl.BlockSpec((w, B_W), imap)

    out_spec = pl.BlockSpec((w, B_W), lambda b, n, r: (b * nblk + n, r))
    o, lse = pl.pallas_call(
        _dil_prompt_kernel,
        grid=(nbatch, nblk, dil),
        in_specs=[spec(cb0, False), spec(cb0 + 1, False), spec(cb0 + 2, False),
                  spec(cb0 + 1, True), spec(cb0 + 2, True)],
        out_specs=[out_spec, out_spec],
        out_shape=[jax.ShapeDtypeStruct((n_prompt // dil, dil * B_W), F32)] * 2,
        compiler_params=_params(("parallel", "arbitrary", "arbitrary"), 32),
        name=f"dilated_prompt_w{win}",
    )(uv, uv, uv, uv, uv)
    return o.reshape(n_prompt, B_W), lse.reshape(n_prompt, B_W)


def _dil_sample_kernel(q_ref, k_ref, v_ref, buf_ref, nbuf_ref, o_ref, l_ref, *, dil):
    lb = buf_ref.shape[2]
    t_new = q_ref.shape[0]
    nrow = DIL_HEADS * t_new
    q = q_ref[...]
    lane_head = lax.broadcasted_iota(jnp.int32, (t_new, B_W), 1) // DIL_HDIM
    qbd = jnp.concatenate([jnp.where(lane_head == h, q, 0.0) for h in range(DIL_HEADS)], axis=0).astype(BF16)
    scale = DIL_HDIM ** -0.5

    old = buf_ref[0]
    k_old = old[0:B_W].astype(BF16)
    v_old = old[B_W:2 * B_W].astype(BF16)
    k_new = k_ref[...].astype(BF16)
    v_new = v_ref[...].astype(BF16)

    tq = lax.broadcasted_iota(jnp.int32, (nrow, lb), 0) % t_new
    d_old = lax.broadcasted_iota(jnp.int32, (nrow, lb), 1) - tq
    valid_m = (d_old >= 0) & ((d_old & (dil - 1)) == 0)
    th = lax.broadcasted_iota(jnp.int32, (nrow, t_new), 0) % t_new
    d_new = th - lax.broadcasted_iota(jnp.int32, (nrow, t_new), 1)
    valid_h = (d_new >= 0) & ((d_new & (dil - 1)) == 0)

    s_m = jnp.where(valid_m, _nn(qbd, k_old) * scale, NEG_INF)
    s_h = jnp.where(valid_h, _nt(qbd, k_new) * scale, NEG_INF)
    m = jnp.maximum(jnp.max(s_m, axis=-1, keepdims=True), jnp.max(s_h, axis=-1, keepdims=True))
    e_m = jnp.exp(s_m - m)
    e_h = jnp.exp(s_h - m)
    l = jnp.sum(e_m, axis=-1, keepdims=True) + jnp.sum(e_h, axis=-1, keepdims=True)
    o_all = (_nt(e_m.astype(BF16), v_old) + _nn(e_h.astype(BF16), v_new)) / l

    nbuf_ref[0] = pltpu.roll(old, lb - t_new, 1)
    kv = jnp.concatenate([k_ref[...], v_ref[...]], axis=1)
    kv_t = jnp.concatenate([kv, jnp.zeros((LANES - t_new, 2 * B_W), F32)], axis=0).T
    lane = lax.broadcasted_iota(jnp.int32, (2 * B_W, LANES), 1)
    tail = jnp.where(lane >= LANES - t_new, pltpu.roll(kv_t, LANES - t_new, 1), nbuf_ref[0, :, lb - LANES:lb])
    nbuf_ref[0, :, lb - LANES:lb] = tail
    lse_all = jnp.broadcast_to(m + jnp.log(l), (nrow, B_W))
    o = jnp.zeros((t_new, B_W), F32)
    lse = jnp.zeros((t_new, B_W), F32)
    for h in range(DIL_HEADS):
        r = slice(h * t_new, (h + 1) * t_new)
        o = jnp.where(lane_head == h, o_all[r], o)
        lse = jnp.where(lane_head == h, lse_all[r], lse)
    o_ref[...] = o
    l_ref[...] = lse


def _dil_sample(u, buf, grp, *, n_prompt, t_new):
    win, dil = DIL_GROUPS[grp]
    nb, lb = buf.shape[0], buf.shape[1]
    assert lb == win and t_new == SUBLANES and n_prompt % t_new == 0
    cb0 = (2 * A_QF + 2 * A_IV) // B_W + 3 * grp
    rb0 = n_prompt // t_new
    bufv = jnp.transpose(buf, (0, 2, 3, 4, 1)).reshape(nb, 2 * B_W, lb)

    def uspec(cb):
        return pl.BlockSpec((t_new, B_W), lambda b: (rb0 + b, cb))

    tok_spec = pl.BlockSpec((t_new, B_W), lambda b: (b, 0))
    nbuf, o, lse = pl.pallas_call(
        functools.partial(_dil_sample_kernel, dil=dil),
        grid=(nb,),
        in_specs=[uspec(cb0), uspec(cb0 + 1), uspec(cb0 + 2),
                  pl.BlockSpec((1, 2 * B_W, lb), lambda b: (b, 0, 0))],
        out_specs=[pl.BlockSpec((1, 2 * B_W, lb), lambda b: (b, 0, 0)), tok_spec, tok_spec],
        out_shape=[jax.ShapeDtypeStruct(bufv.shape, F32),
                   jax.ShapeDtypeStruct((nb * t_new, B_W), F32),
                   jax.ShapeDtypeStruct((nb * t_new, B_W), F32)],
        compiler_params=_params(("parallel",), 48),
        name=f"dilated_sample_w{win}",
    )(u, u, u, bufv)
    nbuf = jnp.transpose(nbuf.reshape(nb, 2, DIL_HEADS, DIL_HDIM, lb), (0, 4, 1, 2, 3))
    return nbuf, o, lse


def _two_source_specs(width, tm, npb):
    return [pl.BlockSpec((tm, width), lambda i: (jnp.minimum(i, npb - 1), 0)),
            pl.BlockSpec((tm, width), lambda i: (jnp.maximum(i - npb, 0), 0))]


def _even_out_kernel(*refs, npb):
    srcs = refs[:14]
    x_ref, w_ref, g_ref, b_ref, y_ref = refs[14:]

    def compute(ga_ref, os_, ls_):
        lses = [l[...] for l in ls_]
        mx = jnp.maximum(jnp.maximum(lses[0], lses[1]), lses[2])
        es = [jnp.exp(l - mx) for l in lses]
        tot = es[0] + es[1] + es[2]
        ob = (es[0] / tot) * os_[0][...] + (es[1] / tot) * os_[1][...] + (es[2] / tot) * os_[2][...]
        m = _nn(ga_ref[...].astype(BF16), w_ref[0:A_IV, :]) + _nn(ob.astype(BF16), w_ref[A_IV:A_IV + B_W, :])
        y_ref[...] = _layer_norm_rows(DN_ALPHA * x_ref[...] + m, g_ref[...], b_ref[...])

    i = pl.program_id(0)
    for src, cond in ((0, i < npb), (1, i >= npb)):
        @pl.when(cond)
        def _(src=src):
            compute(srcs[src], srcs[2 + src:8:2], srcs[8 + src:14:2])


def _even_out(gated, os_, lses, x, w_bf16, g, b, *, n_prompt, tm=512):
    n = x.shape[0]
    assert n % tm == 0 and n_prompt % tm == 0
    npb = n_prompt // tm
    const = lambda shape: pl.BlockSpec(shape, lambda i: (0, 0))
    in_specs = _two_source_specs(A_IV, tm, npb)
    args = list(gated)
    for pair in list(os_) + list(lses):
        in_specs += _two_source_specs(B_W, tm, npb)
        args += list(pair)
    in_specs += [pl.BlockSpec((tm, D_MODEL), lambda i: (i, 0)), const(w_bf16.shape),
                 const((1, D_MODEL)), const((1, D_MODEL))]
    return pl.pallas_call(
        functools.partial(_even_out_kernel, npb=npb),
        grid=(n // tm,),
        in_specs=in_specs,
        out_specs=pl.BlockSpec((tm, D_MODEL), lambda i: (i, 0)),
        out_shape=jax.ShapeDtypeStruct((n, D_MODEL), F32),
        compiler_params=_params(("arbitrary",), 40),
        name="even_out_deepnorm",
    )(*args, x, w_bf16, g, b)


def _router_kernel(x_ref, w_ref, b_ref, e_ref, p_ref):
    logits = _nn(x_ref[...].astype(BF16), w_ref[...]) + b_ref[...]
    tm = logits.shape[0]
    lane = lax.broadcasted_iota(jnp.int32, logits.shape, 1)
    big = jnp.int32(LANES)
    is_g = lane < MOE_GROUPS
    gl = jnp.where(is_g, logits, NEG_INF)
    gmax = jnp.max(gl, axis=-1, keepdims=True)
    gsel = jnp.min(jnp.where(is_g & (gl == gmax), lane, big), axis=-1, keepdims=True)
    gw = 1.0 / jnp.sum(jnp.where(is_g, jnp.exp(gl - gmax), 0.0), axis=-1, keepdims=True)
    e_id = lane - MOE_GROUPS
    in_grp = (e_id >= gsel * MOE_EXPERTS_PER_GROUP) & (e_id < (gsel + 1) * MOE_EXPERTS_PER_GROUP)
    el = jnp.where(in_grp, logits, NEG_INF)
    v1 = jnp.max(el, axis=-1, keepdims=True)
    i1 = jnp.min(jnp.where(in_grp & (el == v1), e_id, big), axis=-1, keepdims=True)
    el2 = jnp.where(e_id == i1, NEG_INF, el)
    v2 = jnp.max(el2, axis=-1, keepdims=True)
    i2 = jnp.min(jnp.where(in_grp & (e_id != i1) & (el2 == v2), e_id, big), axis=-1, keepdims=True)
    e2 = jnp.exp(v2 - v1)
    den = 1.0 + e2
    w1 = gw * (1.0 / den)
    w2 = gw * (e2 / den)
    e_ref[...] = jnp.where(lane == 0, i1, jnp.where(lane == 1, i2, 0))
    p_ref[...] = jnp.where(lane == 0, w1, jnp.where(lane == 1, w2, 0.0))
    del tm


def _router(x, w_bf16, bias, *, tm=512):
    n = x.shape[0]
    row = lambda dt: pl.BlockSpec((tm, LANES), lambda i: (i, 0))
    return pl.pallas_call(
        _router_kernel,
        grid=(n // tm,),
        in_specs=[pl.BlockSpec((tm, D_MODEL), lambda i: (i, 0)),
                  pl.BlockSpec((D_MODEL, LANES), lambda i: (0, 0)),
                  pl.BlockSpec((1, LANES), lambda i: (0, 0))],
        out_specs=[row(jnp.int32), row(F32)],
        out_shape=[jax.ShapeDtypeStruct((n, LANES), jnp.int32), jax.ShapeDtypeStruct((n, LANES), F32)],
        compiler_params=_params(("parallel",), 32),
        name="moe_router",
    )(x, w_bf16, bias)


def _expert_kernel(blk_e_ref, src_ref, dst_ref, x_hbm, wg_ref, wu_ref, wd_ref, y_hbm,
                   xg, yo, gsem, ssem, *, rows):
    del blk_e_ref
    i = pl.program_id(0)
    nsteps = pl.num_programs(0)
    slot = i % 2

    def gather_start(step, sl):
        def body(r, c):
            pltpu.make_async_copy(x_hbm.at[pl.ds(src_ref[step, r], 1)], xg.at[sl, pl.ds(r, 1)],
                                  gsem.at[sl]).start()
            return c
        lax.fori_loop(0, rows, body, 0, unroll=8)

    def scatter_start(step, sl):
        def body(r, c):
            pltpu.make_async_copy(yo.at[sl, pl.ds(r, 1)], y_hbm.at[pl.ds(dst_ref[step, r], 1)],
                                  ssem.at[sl]).start()
            return c
        lax.fori_loop(0, rows, body, 0, unroll=8)

    def gather_wait(sl):
        pltpu.make_async_copy(x_hbm.at[pl.ds(0, rows)], xg.at[sl], gsem.at[sl]).wait()

    def scatter_wait(sl):
        pltpu.make_async_copy(yo.at[sl], y_hbm.at[pl.ds(0, rows)], ssem.at[sl]).wait()

    @pl.when(i == 0)
    def _():
        gather_start(0, 0)
        spare = y_hbm.shape[0] - 2 * rows
        for sl in range(2):
            yo[sl] = jnp.zeros((rows, D_MODEL), F32)
            fill = pltpu.make_async_copy(yo.at[sl], y_hbm.at[pl.ds(spare + sl * rows, rows)], ssem.at[sl])
            fill.start()
            fill.wait()

    @pl.when(i + 1 < nsteps)
    def _():
        gather_start(i + 1, 1 - slot)

    gather_wait(slot)

    @pl.when(i >= 2)
    def _():
        scatter_wait(slot)

    xb = xg[slot].astype(BF16)
    hdn = _nn(xb, wg_ref[0])
    hdn = (hdn * jax.nn.sigmoid(hdn)) * _nn(xb, wu_ref[0])
    yo[slot] = _nn(hdn.astype(BF16), wd_ref[0])
    scatter_start(i, slot)

    @pl.when(i == nsteps - 1)
    def _():
        scatter_wait(slot)

        @pl.when(nsteps >= 2)
        def _():
            scatter_wait(1 - slot)


def _experts(x, blk_e, src, dst, wg, wu, wd, *, rows):
    n = x.shape[0]
    nblk = src.shape[0]
    wspec = lambda shape: pl.BlockSpec((1,) + shape, lambda i, be, s, d: (be[i], 0, 0))
    grid_spec = pltpu.PrefetchScalarGridSpec(
        num_scalar_prefetch=3,
        grid=(nblk,),
        in_specs=[pl.BlockSpec(memory_space=pl.ANY),
                  wspec((D_MODEL, MOE_FF)), wspec((D_MODEL, MOE_FF)), wspec((MOE_FF, D_MODEL))],
        out_specs=pl.BlockSpec(memory_space=pl.ANY),
        scratch_shapes=[pltpu.VMEM((2, rows, D_MODEL), F32), pltpu.VMEM((2, rows, D_MODEL), F32),
                        pltpu.SemaphoreType.DMA((2,)), pltpu.SemaphoreType.DMA((2,))],
    )
    return pl.pallas_call(
        functools.partial(_expert_kernel, rows=rows),
        grid_spec=grid_spec,
        out_shape=jax.ShapeDtypeStruct((MOE_TOPK * n + 2 * rows, D_MODEL), F32),
        compiler_params=_params(("arbitrary",), 40),
        name="moe_experts",
    )(blk_e, src, dst, x, wg, wu, wd)


def _moe_combine_kernel(y0_ref, y1_ref, p_ref, x_ref, g_ref, b_ref, o_ref):
    p = p_ref[...]
    f = y0_ref[...] * p[:, 0:1] + y1_ref[...] * p[:, 1:2]
    o_ref[...] = _layer_norm_rows(DN_ALPHA * x_ref[...] + f, g_ref[...], b_ref[...])


def _moe_combine(y2, p, x, g, b, *, tm=512):
    n = x.shape[0]
    assert n % tm == 0
    return pl.pallas_call(
        _moe_combine_kernel,
        grid=(n // tm,),
        in_specs=[pl.BlockSpec((tm, D_MODEL), lambda i: (i, 0)),
                  pl.BlockSpec((tm, D_MODEL), lambda i: (n // tm + i, 0)),
                  pl.BlockSpec((tm, LANES), lambda i: (i, 0)),
                  pl.BlockSpec((tm, D_MODEL), lambda i: (i, 0)),
                  pl.BlockSpec((1, D_MODEL), lambda i: (0, 0)),
                  pl.BlockSpec((1, D_MODEL), lambda i: (0, 0))],
        out_specs=pl.BlockSpec((tm, D_MODEL), lambda i: (i, 0)),
        out_shape=jax.ShapeDtypeStruct((n, D_MODEL), F32),
        compiler_params=_params(("parallel",), 40),
        name="moe_combine_deepnorm",
    )(y2, y2, p, x, g, b)


def _dispatch_plan(eidx, *, rows):
    n = eidx.shape[0]
    na = n * MOE_TOPK
    flat_e = eidx.T.reshape(-1)
    order = jnp.argsort(flat_e).astype(jnp.int32)
    se = flat_e[order]
    counts = jnp.bincount(flat_e, length=MOE_EXPERTS).astype(jnp.int32)
    padded = (counts + rows - 1) // rows * rows
    pad_end = jnp.cumsum(padded)
    pad_start = pad_end - padded
    start = jnp.cumsum(counts) - counts
    dest = pad_start[se] + jnp.arange(na, dtype=jnp.int32) - start[se]
    nblk = -(-(na + MOE_EXPERTS * (rows - 1)) // rows)
    dst = jnp.full((nblk * rows,), -1, jnp.int32).at[dest].set(order)
    pad = dst < 0
    src = jnp.where(pad, 0, dst % n)
    dst = jnp.where(pad, na + jnp.arange(nblk * rows, dtype=jnp.int32) % (2 * rows), dst)
    blk_e = jnp.minimum(jnp.searchsorted(pad_end, jnp.arange(nblk, dtype=jnp.int32) * rows, side="right"),
                        MOE_EXPERTS - 1).astype(jnp.int32)
    return blk_e, src.reshape(nblk, rows), dst.reshape(nblk, rows)


def _hier_moe(x, wr_bf16, br, wg, wu, wd, ln_g, ln_b, *, rows=256):
    e_pad, p_pad = _router(x, wr_bf16, br)
    blk_e, src, dst = _dispatch_plan(e_pad[:, :MOE_TOPK], rows=rows)
    y2 = _experts(x, blk_e, src, dst, wg, wu, wd, rows=rows)
    return _moe_combine(y2, p_pad, x, ln_g, ln_b)


def _router_weights(wg, bg, we, be):
    w = jnp.zeros((D_MODEL, LANES), F32).at[:, :MOE_GROUPS].set(wg).at[:, MOE_GROUPS:MOE_GROUPS + MOE_EXPERTS].set(we)
    b = jnp.zeros((1, LANES), F32).at[0, :MOE_GROUPS].set(bg).at[0, MOE_GROUPS:MOE_GROUPS + MOE_EXPERTS].set(be)
    return w.astype(BF16), b


def _even_layer(x, state_hgrn, caches, w_in, lb_logits, norm_g, w_out, ln_g, ln_b, *, n_prompt, seq, t_new):
    n_tok = x.shape[0]
    nbatch = n_prompt // seq
    cb0 = 2 * A_QF + 2 * A_IV
    views = [(cb0 + 3 * B_W * grp, 3 * B_W, dil) for grp, (_, dil) in enumerate(DIL_GROUPS) if dil > 1]
    u, *u_res = _matmul(x, w_in.astype(BF16), residue_views=views)
    lb = jnp.cumsum(jax.nn.softmax(lb_logits.astype(F32), axis=0), axis=0)[0].reshape(1, A_QF)
    ng = norm_g.reshape(1, HGRN_VDIM)
    zeros = jnp.zeros((nbatch, HGRN_HEADS, HGRN_KDIM, HGRN_VDIM), F32)
    gated_p, hgrn_p = _hgrn(u, lb, ng, zeros, row0=0, seq=seq, tb=512, c=HGRN_CHUNK)
    gated_s, hgrn_s = _hgrn(u, lb, ng, state_hgrn, row0=n_prompt, seq=t_new, tb=t_new, c=t_new)
    os_, lses, swa_p, swa_s = [], [], [], []
    u_res = iter(u_res)
    for grp, (win, dil) in enumerate(DIL_GROUPS):
        if dil > 1:
            o_p, lse_p = _dil_prompt(next(u_res), grp, n_prompt=n_prompt, seq=seq, ncb=3, cb0=0)
        else:
            o_p, lse_p = _dil_prompt(u, grp, n_prompt=n_prompt, seq=seq, ncb=EVEN_IN // B_W,
                                     cb0=cb0 // B_W + 3 * grp)
        nbuf, o_s, lse_s = _dil_sample(u, caches[grp], grp, n_prompt=n_prompt, t_new=t_new)
        os_.append((o_p, o_s))
        lses.append((lse_p, lse_s))
        swa_s.append(nbuf)
        kv = u[:n_prompt, cb0 + 3 * B_W * grp + B_W:cb0 + 3 * B_W * (grp + 1)]
        kv = kv.reshape(nbatch, seq, 2, DIL_HEADS, DIL_HDIM)
        swa_p.append(kv[:, seq - min(win, seq):])
    y = _even_out((gated_p, gated_s), os_, lses, x, w_out.astype(BF16), ln_g.reshape(1, -1), ln_b.reshape(1, -1),
                  n_prompt=n_prompt)
    return y, hgrn_p, hgrn_s, swa_p, swa_s


HEAD_W = LANES
ROPE_LO = MLA_NOPE
ROPE_MID = MLA_NOPE + MLA_ROPE // 2
ROPE_HI = MLA_NOPE + MLA_ROPE


def _rms_rows(x, g):
    return x * lax.rsqrt(jnp.mean(x * x, axis=-1, keepdims=True) + RMS_EPS) * g


def _odd_prep_kernel(u_ref, cos_ref, sin_ref, gq_ref, gkv_ref, wqa_ref, wqb_ref, wk_ref, wv_ref,
                     lat_ref, q_ref, k_ref, v_ref):
    ckv = u_ref[:, 0:MLA_KV_LORA]
    krb = u_ref[:, MLA_KV_LORA:MLA_KV_LORA + HEAD_W]
    cq = u_ref[:, MLA_KV_LORA + HEAD_W:MLA_KV_LORA + HEAD_W + MLA_Q_LORA]
    cos = cos_ref[...]
    sin = sin_ref[...]
    cos_h = jnp.concatenate([cos] * MLA_HEADS, axis=1)
    sin_h = jnp.concatenate([sin] * MLA_HEADS, axis=1)
    cqn = _rms_rows(cq, gq_ref[...]).astype(BF16)
    q_ref[...] = (_nn(cqn, wqa_ref[...]) * cos_h + _nn(cqn, wqb_ref[...]) * sin_h).astype(BF16)
    c = _rms_rows(ckv, gkv_ref[...])
    lane = lax.broadcasted_iota(jnp.int32, krb.shape, 1)
    swapped = jnp.where(lane < ROPE_MID, pltpu.roll(krb, HEAD_W - MLA_ROPE // 2, 1), pltpu.roll(krb, MLA_ROPE // 2, 1))
    rot = krb * cos + swapped * sin
    lat_ref[:, 0:MLA_KV_LORA] = c
    lat_ref[:, MLA_KV_LORA:MLA_LAT] = rot[:, ROPE_LO:ROPE_HI]
    cb = c.astype(BF16)
    k_ref[...] = (_nn(cb, wk_ref[...]) + jnp.concatenate([rot] * MLA_HEADS, axis=1)).astype(BF16)
    v_ref[...] = _nn(cb, wv_ref[...]).astype(BF16)


def _odd_prep(u_a, cos_t, sin_t, gq, gkv, wqa, wqb, wk, wv, *, tm=256):
    n = u_a.shape[0]
    row = lambda w_: pl.BlockSpec((tm, w_), lambda i: (i, 0))
    const = lambda a: pl.BlockSpec(a.shape, lambda i: (0, 0))
    hw = MLA_HEADS * HEAD_W
    return pl.pallas_call(
        _odd_prep_kernel,
        grid=(n // tm,),
        in_specs=[row(u_a.shape[1]), row(HEAD_W), row(HEAD_W), const(gq), const(gkv),
                  const(wqa), const(wqb), const(wk), const(wv)],
        out_specs=[row(MLA_LAT), row(hw), row(hw), row(MLA_HEADS * MLA_VDIM)],
        out_shape=[jax.ShapeDtypeStruct((n, MLA_LAT), F32), jax.ShapeDtypeStruct((n, hw), BF16),
                   jax.ShapeDtypeStruct((n, hw), BF16), jax.ShapeDtypeStruct((n, MLA_HEADS * MLA_VDIM), BF16)],
        compiler_params=_params(("parallel",), 40),
        name="odd_prep",
    )(u_a, cos_t, sin_t, gq, gkv, wqa, wqb, wk, wv)


def _odd_weights(w_in, w_uq, w_uk, w_uv):
    o = 0
    cq_w = w_in[:, o:o + MLA_Q_LORA]; o += MLA_Q_LORA
    ckv_w = w_in[:, o:o + MLA_KV_LORA]; o += MLA_KV_LORA
    kr_w = w_in[:, o:o + MLA_ROPE]; o += MLA_ROPE
    rest = w_in[:, o:]
    zeros = lambda w_: jnp.zeros((D_MODEL, w_), w_in.dtype)
    w_in2 = jnp.concatenate([ckv_w, zeros(ROPE_LO), kr_w, zeros(HEAD_W - ROPE_HI), cq_w, rest], axis=1)

    half = MLA_ROPE // 2
    nope = w_uq[:, :, :MLA_NOPE]
    x1 = w_uq[:, :, MLA_NOPE:MLA_NOPE + half]
    x2 = w_uq[:, :, MLA_NOPE + half:]
    pad = jnp.zeros((MLA_Q_LORA, MLA_HEADS, HEAD_W - ROPE_HI), w_uq.dtype)
    wqa = jnp.concatenate([nope, x1, x2, pad], axis=2).reshape(MLA_Q_LORA, MLA_HEADS * HEAD_W)
    wqb = jnp.concatenate([jnp.zeros_like(nope), x2, x1, pad], axis=2).reshape(MLA_Q_LORA, MLA_HEADS * HEAD_W)
    kpad = jnp.zeros((MLA_KV_LORA, MLA_HEADS, HEAD_W - MLA_NOPE), w_uk.dtype)
    wk = jnp.concatenate([w_uk, kpad], axis=2).reshape(MLA_KV_LORA, MLA_HEADS * HEAD_W)
    wv = w_uv.reshape(MLA_KV_LORA, MLA_HEADS * MLA_VDIM)
    up = jnp.zeros((MLA_HEADS, HEAD_W, MLA_LAT), F32)
    up = up.at[:, :MLA_NOPE, :MLA_KV_LORA].set(jnp.transpose(w_uk, (1, 2, 0)))
    up = up.at[:, ROPE_LO:ROPE_HI, MLA_KV_LORA:].set(jnp.broadcast_to(jnp.eye(MLA_ROPE, dtype=F32), (MLA_HEADS, MLA_ROPE, MLA_ROPE)))
    wvh = jnp.transpose(w_uv, (1, 0, 2))
    return (w_in2.astype(BF16), wqa.astype(BF16), wqb.astype(BF16), wk.astype(BF16), wv.astype(BF16),
            up.astype(BF16), wvh.astype(BF16))


def _rope_tables(pos):
    half = MLA_ROPE // 2
    inv = ROPE_THETA ** (-jnp.arange(half, dtype=F32) / half)
    ang = pos.astype(F32)[:, None] * inv[None, :]
    cos, sin = jnp.cos(ang), jnp.sin(ang)
    n = pos.shape[0]
    cos_t = jnp.concatenate([jnp.ones((n, ROPE_LO), F32), cos, cos, jnp.zeros((n, HEAD_W - ROPE_HI), F32)], axis=1)
    sin_t = jnp.concatenate([jnp.zeros((n, ROPE_LO), F32), -sin, sin, jnp.zeros((n, HEAD_W - ROPE_HI), F32)], axis=1)
    return cos_t, sin_t


BIG_NEG = -1e30


def _causal_tiles(nq):
    qs = [i for i in range(nq) for _ in range(i + 1)]
    ks = [j for i in range(nq) for j in range(i + 1)]
    return jnp.asarray(qs, jnp.int32), jnp.asarray(ks, jnp.int32)


def _mla_flash_kernel(qt_ref, kt_ref, q_ref, k_ref, v_ref, o_ref, m_scr, l_scr, acc_scr):
    step = pl.program_id(2)
    qi = qt_ref[step]
    ki = kt_ref[step]
    tq, tk = q_ref.shape[0], k_ref.shape[0]
    scale = (MLA_NOPE + MLA_ROPE) ** -0.5

    @pl.when(ki == 0)
    def _():
        m_scr[...] = jnp.full(m_scr.shape, BIG_NEG, F32)
        l_scr[...] = jnp.zeros(l_scr.shape, F32)
        acc_scr[...] = jnp.zeros(acc_scr.shape, F32)

    nfold = tk // LANES

    def update(masked):
        v = v_ref[...]
        if masked:
            row = lax.broadcasted_iota(jnp.int32, (tq, tk), 0)
            col = lax.broadcasted_iota(jnp.int32, (tq, tk), 1)
            keep = col <= row
        for hh in range(2):
            cs = slice(HEAD_W * hh, HEAD_W * (hh + 1))
            s = _nt(q_ref[:, cs], k_ref[:, cs]) * scale
            if masked:
                s = jnp.where(keep, s, BIG_NEG)
            s_fold = s[:, 0:LANES]
            for i in range(1, nfold):
                s_fold = jnp.maximum(s_fold, s[:, i * LANES:(i + 1) * LANES])
            m_old = m_scr[hh]
            m_new = jnp.maximum(m_old, jnp.max(s_fold, axis=-1, keepdims=True))
            alpha = jnp.exp(m_old - m_new)
            p = jnp.exp(s - jnp.concatenate([m_new] * nfold, axis=1))
            p_fold = p[:, 0:LANES]
            for i in range(1, nfold):
                p_fold = p_fold + p[:, i * LANES:(i + 1) * LANES]
            l_scr[hh] = alpha * l_scr[hh] + jnp.sum(p_fold, axis=-1, keepdims=True)
            acc_scr[hh] = alpha * acc_scr[hh] + _nn(p.astype(BF16), v)
            m_scr[hh] = m_new

    @pl.when(ki < qi)
    def _():
        update(False)

    @pl.when(ki == qi)
    def _():
        update(True)

    @pl.when(ki == qi)
    def _():
        lane = lax.broadcasted_iota(jnp.int32, (tq, 2 * MLA_VDIM), 1)
        o = jnp.where(lane < MLA_VDIM, acc_scr[0] / l_scr[0], acc_scr[1] / l_scr[1])
        o_ref[...] = o.astype(o_ref.dtype)


def _mla_prompt(q, k, v, *, n_prompt, seq, tq=512):
    nbatch = n_prompt // seq
    nq = seq // tq
    npair = MLA_HEADS // 2
    qt, kt = _causal_tiles(nq)
    grid_spec = pltpu.PrefetchScalarGridSpec(
        num_scalar_prefetch=2,
        grid=(nbatch, npair, qt.shape[0]),
        in_specs=[pl.BlockSpec((tq, 2 * HEAD_W), lambda b, h, s, qt, kt: (b * nq + qt[s], h)),
                  pl.BlockSpec((tq, 2 * HEAD_W), lambda b, h, s, qt, kt: (b * nq + kt[s], h)),
                  pl.BlockSpec((tq, 2 * MLA_VDIM), lambda b, h, s, qt, kt: (b * nq + kt[s], h))],
        out_specs=pl.BlockSpec((tq, 2 * MLA_VDIM), lambda b, h, s, qt, kt: (b * nq + qt[s], h)),
        scratch_shapes=[pltpu.VMEM((2, tq, LANES), F32), pltpu.VMEM((2, tq, LANES), F32),
                        pltpu.VMEM((2, tq, 2 * MLA_VDIM), F32)],
    )
    return pl.pallas_call(
        _mla_flash_kernel,
        grid_spec=grid_spec,
        out_shape=jax.ShapeDtypeStruct((n_prompt, MLA_HEADS * MLA_VDIM), BF16),
        compiler_params=_params(("parallel", "parallel", "arbitrary"), 32),
        name="mla_prompt_attention",
    )(qt, kt, q, k, v)


def _absorb_kernel(q_ref, up_ref, o_ref):
    for h in range(MLA_HEADS):
        o_ref[h] = _nn(q_ref[:, HEAD_W * h:HEAD_W * (h + 1)], up_ref[h])


def _absorb(q, up, *, row0, nrows, tm=128):
    assert row0 % tm == 0 and nrows % tm == 0
    return pl.pallas_call(
        _absorb_kernel,
        grid=(nrows // tm,),
        in_specs=[pl.BlockSpec((tm, MLA_HEADS * HEAD_W), lambda i: (row0 // tm + i, 0)),
                  pl.BlockSpec(up.shape, lambda i: (0, 0, 0))],
        out_specs=pl.BlockSpec((MLA_HEADS, tm, MLA_LAT), lambda i: (0, i, 0)),
        out_shape=jax.ShapeDtypeStruct((MLA_HEADS, nrows, MLA_LAT), F32),
        compiler_params=_params(("parallel",), 32),
        name="mla_absorb_query",
    )(q, up)


def _page_copy(cache_hbm, buf, sem, page, slot, j):
    return pltpu.make_async_copy(cache_hbm.at[page], buf.at[slot, j], sem.at[slot])


def _mla_sample_kernel(pt_ref, q_ref, new_ref, cache_hbm, o_ref, buf, sem, *, pg, nchunk):
    b = pl.program_id(0)
    t_new = new_ref.shape[0]
    nrow = MLA_HEADS * t_new
    scale = (MLA_NOPE + MLA_ROPE) ** -0.5

    def start(chunk, slot):
        for j in range(pg):
            _page_copy(cache_hbm, buf, sem, pt_ref[b, chunk * pg + j], slot, j).start()

    def wait(slot):
        for j in range(pg):
            _page_copy(cache_hbm, buf, sem, 0, slot, j).wait()

    start(0, 0)
    q = q_ref[...].reshape(nrow, MLA_LAT).astype(BF16)

    def body(c, carry):
        m, l, acc = carry
        slot = c % 2

        @pl.when(c + 1 < nchunk)
        def _():
            start(c + 1, 1 - slot)

        wait(slot)
        pages = [buf[slot, j].astype(BF16) for j in range(pg)]
        ss = [_nn(q, kc) * scale for kc in pages]
        smax = ss[0]
        for sj in ss[1:]:
            smax = jnp.maximum(smax, sj)
        m_new = jnp.maximum(m, jnp.max(smax, axis=-1, keepdims=True))
        alpha = jnp.exp(m - m_new)
        psum = jnp.zeros((nrow, PAGE_SIZE), F32)
        pv = jnp.zeros((nrow, MLA_KV_LORA), F32)
        for sj, kc in zip(ss, pages):
            p = jnp.exp(sj - m_new)
            psum = psum + p
            pv = pv + _nt(p.astype(BF16), kc[0:MLA_KV_LORA, :])
        l = alpha * l + jnp.sum(psum, axis=-1, keepdims=True)
        acc = alpha * acc + pv
        return m_new, l, acc

    init = (jnp.full((nrow, 1), BIG_NEG, F32), jnp.zeros((nrow, 1), F32), jnp.zeros((nrow, MLA_KV_LORA), F32))
    m, l, acc = lax.fori_loop(0, nchunk, body, init)

    kn = new_ref[...].astype(BF16)
    tq = lax.broadcasted_iota(jnp.int32, (nrow, t_new), 0) % t_new
    tk = lax.broadcasted_iota(jnp.int32, (nrow, t_new), 1)
    s = jnp.where(tk <= tq, _nt(q, kn) * scale, BIG_NEG)
    m_new = jnp.maximum(m, jnp.max(s, axis=-1, keepdims=True))
    alpha = jnp.exp(m - m_new)
    p = jnp.exp(s - m_new)
    l = alpha * l + jnp.sum(p, axis=-1, keepdims=True)
    acc = alpha * acc + _nn(p.astype(BF16), kn[:, 0:MLA_KV_LORA])
    o_ref[...] = (acc / l).reshape(MLA_HEADS, t_new, MLA_KV_LORA)


def _mla_sample(q_abs, lat, cache, page_table, *, row0, t_new, pg=16):
    nb, npages = page_table.shape
    assert npages % pg == 0 and t_new == SUBLANES and row0 % t_new == 0
    cache = jnp.transpose(cache, (0, 2, 1))
    grid_spec = pltpu.PrefetchScalarGridSpec(
        num_scalar_prefetch=1,
        grid=(nb,),
        in_specs=[pl.BlockSpec((MLA_HEADS, t_new, MLA_LAT), lambda b, pt: (0, b, 0)),
                  pl.BlockSpec((t_new, MLA_LAT), lambda b, pt: (row0 // t_new + b, 0)),
                  pl.BlockSpec(memory_space=pl.ANY)],
        out_specs=pl.BlockSpec((MLA_HEADS, t_new, MLA_KV_LORA), lambda b, pt: (0, b, 0)),
        scratch_shapes=[pltpu.VMEM((2, pg, MLA_LAT, PAGE_SIZE), F32), pltpu.SemaphoreType.DMA((2,))],
    )
    return pl.pallas_call(
        functools.partial(_mla_sample_kernel, pg=pg, nchunk=npages // pg),
        grid_spec=grid_spec,
        out_shape=jax.ShapeDtypeStruct((MLA_HEADS, nb * t_new, MLA_KV_LORA), F32),
        compiler_params=_params(("arbitrary",), 40),
        name="mla_sample_attention",
    )(page_table, q_abs, lat, cache)


def _latent_up_kernel(o_ref, w_ref, y_ref):
    outs = [_nn(o_ref[h].astype(BF16), w_ref[h]) for h in range(MLA_HEADS)]
    y_ref[...] = jnp.concatenate(outs, axis=1).astype(y_ref.dtype)


def _latent_up(o_lat, wvh, *, tm=128):
    nrows = o_lat.shape[1]
    return pl.pallas_call(
        _latent_up_kernel,
        grid=(nrows // tm,),
        in_specs=[pl.BlockSpec((MLA_HEADS, tm, MLA_KV_LORA), lambda i: (0, i, 0)),
                  pl.BlockSpec(wvh.shape, lambda i: (0, 0, 0))],
        out_specs=pl.BlockSpec((tm, MLA_HEADS * MLA_VDIM), lambda i: (i, 0)),
        out_shape=jax.ShapeDtypeStruct((nrows, MLA_HEADS * MLA_VDIM), BF16),
        compiler_params=_params(("parallel",), 32),
        name="mla_latent_up",
    )(o_lat, wvh)


def _block_mean_kernel(k_ref, o_ref, *, nblk):
    k = k_ref[...].reshape(nblk, MOBA_BLOCK, D_KV)
    o_ref[0, 0:nblk, :] = jnp.mean(k, axis=1)
    if nblk < o_ref.shape[1]:
        o_ref[0, nblk:, :] = jnp.zeros((o_ref.shape[1] - nblk, D_KV), F32)


def _block_means(u_b, *, n_prompt, seq):
    nbatch = n_prompt // seq
    nblk = seq // MOBA_BLOCK
    assert seq % MOBA_BLOCK == 0 and nblk <= LANES
    return pl.pallas_call(
        functools.partial(_block_mean_kernel, nblk=nblk),
        grid=(nbatch,),
        in_specs=[pl.BlockSpec((seq, D_KV), lambda b: (b, D_Q // D_KV))],
        out_specs=pl.BlockSpec((1, LANES, D_KV), lambda b: (b, 0, 0)),
        out_shape=jax.ShapeDtypeStruct((nbatch, LANES, D_KV), F32),
        compiler_params=_params(("parallel",), 32),
        name="moba_block_means",
    )(u_b)


def _top_blocks(gate, limit):
    lane = lax.broadcasted_iota(jnp.int32, gate.shape, 1)
    g = jnp.where(lane < limit, gate, NEG_INF)
    sel = jnp.zeros(gate.shape, F32)
    for j in range(MOBA_TOPK):
        mx = jnp.max(g, axis=-1, keepdims=True)
        idx = jnp.min(jnp.where(g == mx, lane, LANES), axis=-1, keepdims=True)
        hit = lane == idx + jnp.where(j < limit, 0, 2 * LANES)
        sel = jnp.where(hit, 1.0, sel)
        g = jnp.where(lane == idx, NEG_INF, g)
    return sel


def _moba_flash_kernel(qt_ref, kt_ref, q_ref, k_ref, v_ref, km_ref, o_ref, sel_scr, m_scr, l_scr, acc_scr):
    step = pl.program_id(2)
    qi = qt_ref[step]
    ki = kt_ref[step]
    tq, tk = q_ref.shape[0], k_ref.shape[0]
    scale = MOBA_HDIM ** -0.5
    grp = MOBA_HEADS // MOBA_KV_HEADS
    nh = 2 * grp

    @pl.when(ki == 0)
    def _():
        m_scr[...] = jnp.full(m_scr.shape, BIG_NEG, F32)
        l_scr[...] = jnp.zeros(l_scr.shape, F32)
        acc_scr[...] = jnp.zeros(acc_scr.shape, F32)
        for hh in range(nh):
            kv = hh // grp
            qh = q_ref[:, MOBA_HDIM * hh:MOBA_HDIM * (hh + 1)].astype(BF16)
            km = km_ref[0, :, MOBA_HDIM * kv:MOBA_HDIM * (kv + 1)].astype(BF16)
            sel_scr[hh] = _top_blocks(_nt(qh, km), qi)

    row = lax.broadcasted_iota(jnp.int32, (tq, tk), 0)
    col = lax.broadcasted_iota(jnp.int32, (tq, tk), 1)
    causal = col <= row + jnp.where(ki < qi, tk, 0)
    own = jnp.where(ki == qi, 1.0, 0.0)
    spread = jnp.where(lax.broadcasted_iota(jnp.int32, (LANES, LANES), 0) == ki, 1.0, 0.0).astype(BF16)
    nhalf = tk // LANES
    for hh in range(nh):
        kv = hh // grp
        ks = slice(MOBA_HDIM * kv, MOBA_HDIM * (kv + 1))
        qh = q_ref[:, MOBA_HDIM * hh:MOBA_HDIM * (hh + 1)].astype(BF16)
        picked = _nn(sel_scr[hh].astype(BF16), spread) + own
        picked = jnp.concatenate([picked] * nhalf, axis=1)
        keep = causal & (picked > 0.5)
        s = jnp.where(keep, _nt(qh, k_ref[:, ks].astype(BF16)) * scale, BIG_NEG)
        s_fold = s[:, 0:LANES]
        for i in range(1, nhalf):
            s_fold = jnp.maximum(s_fold, s[:, i * LANES:(i + 1) * LANES])
        m_old = m_scr[hh]
        m_new = jnp.maximum(m_old, jnp.max(s_fold, axis=-1, keepdims=True))
        alpha = jnp.exp(m_old - m_new)
        p = jnp.where(keep, jnp.exp(s - jnp.concatenate([m_new] * nhalf, axis=1)), 0.0)
        p_fold = p[:, 0:LANES]
        for i in range(1, nhalf):
            p_fold = p_fold + p[:, i * LANES:(i + 1) * LANES]
        l_scr[hh] = alpha * l_scr[hh] + jnp.sum(p_fold, axis=-1, keepdims=True)
        acc_scr[hh] = alpha[:, 0:MOBA_HDIM] * acc_scr[hh] + _nn(p.astype(BF16), v_ref[:, ks].astype(BF16))
        m_scr[hh] = m_new

    @pl.when(ki == qi)
    def _():
        o_ref[...] = jnp.concatenate([acc_scr[hh] / l_scr[hh][:, 0:MOBA_HDIM] for hh in range(nh)],
                                     axis=1).astype(o_ref.dtype)


def _moba_prompt(u_b, kmean, *, n_prompt, seq):
    nbatch = n_prompt // seq
    tq = MOBA_BLOCK
    nq = seq // tq
    npair = MOBA_KV_HEADS // 2
    grp = MOBA_HEADS // MOBA_KV_HEADS
    qw = 2 * grp * MOBA_HDIM
    kw = 2 * MOBA_HDIM
    kcb = D_Q // kw
    vcb = (D_Q + D_KV) // kw
    qt, kt = _causal_tiles(nq)
    grid_spec = pltpu.PrefetchScalarGridSpec(
        num_scalar_prefetch=2,
        grid=(nbatch, npair, qt.shape[0]),
        in_specs=[pl.BlockSpec((tq, qw), lambda b, h, s, qt, kt: (b * nq + qt[s], h)),
                  pl.BlockSpec((tq, kw), lambda b, h, s, qt, kt: (b * nq + kt[s], kcb + h)),
                  pl.BlockSpec((tq, kw), lambda b, h, s, qt, kt: (b * nq + kt[s], vcb + h)),
                  pl.BlockSpec((1, LANES, kw), lambda b, h, s, qt, kt: (b, 0, h))],
        out_specs=pl.BlockSpec((tq, qw), lambda b, h, s, qt, kt: (b * nq + qt[s], h)),
        scratch_shapes=[pltpu.VMEM((2 * grp, tq, LANES), F32), pltpu.VMEM((2 * grp, tq, LANES), F32),
                        pltpu.VMEM((2 * grp, tq, LANES), F32), pltpu.VMEM((2 * grp, tq, MOBA_HDIM), F32)],
    )
    return pl.pallas_call(
        _moba_flash_kernel,
        grid_spec=grid_spec,
        out_shape=jax.ShapeDtypeStruct((n_prompt, D_Q), BF16),
        compiler_params=_params(("parallel", "parallel", "arbitrary"), 32),
        name="moba_prompt_attention",
    )(qt, kt, u_b, u_b, u_b, kmean)


def _moba_sample_kernel(pt_ref, q_ref, kvn_ref, cache_hbm, o_ref, buf, sem, km_scr, m_scr, l_scr, o_scr,
                        *, pg, nchunk):
    b = pl.program_id(0)
    t_new = q_ref.shape[0]
    grp = MOBA_HEADS // MOBA_KV_HEADS
    nrow = MOBA_HEADS * t_new
    scale = MOBA_HDIM ** -0.5
    bpc = pg * PAGE_SIZE // MOBA_BLOCK
    nblk = nchunk * bpc

    def start(chunk, slot):
        for j in range(pg):
            _page_copy(cache_hbm, buf, sem, pt_ref[b, chunk * pg + j], slot, j).start()

    def wait(slot):
        for j in range(pg):
            _page_copy(cache_hbm, buf, sem, 0, slot, j).wait()

    start(0, 0)
    q = q_ref[...]
    pieces = []
    for h in range(MOBA_HEADS):
        kv = h // grp
        parts = []
        if kv > 0:
            parts.append(jnp.zeros((t_new, MOBA_HDIM * kv), F32))
        parts.append(q[:, MOBA_HDIM * h:MOBA_HDIM * (h + 1)])
        if kv < MOBA_KV_HEADS - 1:
            parts.append(jnp.zeros((t_new, MOBA_HDIM * (MOBA_KV_HEADS - 1 - kv)), F32))
        pieces.append(jnp.concatenate(parts, axis=1))
    qbd = jnp.concatenate(pieces, axis=0).astype(BF16)
    km_scr[...] = jnp.zeros(km_scr.shape, F32)
    m_scr[...] = jnp.full(m_scr.shape, BIG_NEG, F32)
    l_scr[...] = jnp.zeros(l_scr.shape, F32)
    lane_k = lax.broadcasted_iota(jnp.int32, (D_KV, LANES), 1)
    lane_r = lax.broadcasted_iota(jnp.int32, (nrow, LANES), 1)

    def body(c, carry):
        slot = c % 2

        @pl.when(c + 1 < nchunk)
        def _():
            start(c + 1, 1 - slot)

        wait(slot)
        ppb = MOBA_BLOCK // PAGE_SIZE
        for j in range(bpc):
            n = c * bpc + j
            ks = [buf[slot, j * ppb + i, 0:D_KV, :] for i in range(ppb)]
            ksum = ks[0]
            for kf in ks[1:]:
                ksum = ksum + kf
            kmean = jnp.sum(ksum, axis=1, keepdims=True) * (1.0 / MOBA_BLOCK)
            km_scr[...] = jnp.where(lane_k == n, kmean, km_scr[...])
            ss = [_nn(qbd, kf.astype(BF16)) * scale for kf in ks]
            smax = ss[0]
            for si in ss[1:]:
                smax = jnp.maximum(smax, si)
            m = jnp.max(smax, axis=-1, keepdims=True)
            psum = jnp.zeros((nrow, PAGE_SIZE), F32)
            pv = jnp.zeros((nrow, D_KV), F32)
            for i, si in enumerate(ss):
                p = jnp.exp(si - m)
                psum = psum + p
                pv = pv + _nt(p.astype(BF16), buf[slot, j * ppb + i, D_KV:2 * D_KV, :].astype(BF16))
            m_scr[...] = jnp.where(lane_r == n, m, m_scr[...])
            l_scr[...] = jnp.where(lane_r == n, jnp.sum(psum, axis=-1, keepdims=True), l_scr[...])
            o_scr[n] = pv
        return carry

    lax.fori_loop(0, nchunk, body, 0)

    kn = kvn_ref[:, 0:D_KV].astype(BF16)
    vn = kvn_ref[:, D_KV:2 * D_KV].astype(BF16)
    tq = lax.broadcasted_iota(jnp.int32, (nrow, t_new), 0) % t_new
    tk = lax.broadcasted_iota(jnp.int32, (nrow, t_new), 1)
    s_own = jnp.where(tk <= tq, _nt(qbd, kn) * scale, BIG_NEG)
    m_own = jnp.max(s_own, axis=-1, keepdims=True)
    p_own = jnp.exp(s_own - m_own)
    l_own = jnp.sum(p_own, axis=-1, keepdims=True)
    o_own = _nn(p_own.astype(BF16), vn)

    picked = _top_blocks(_nn(qbd, km_scr[...].astype(BF16)), nblk) > 0.5
    m_all = m_scr[...]
    mx = jnp.maximum(jnp.max(jnp.where(picked, m_all, BIG_NEG), axis=-1, keepdims=True), m_own)
    w_all = jnp.where(picked, jnp.exp(m_all - mx), 0.0)
    w_own = jnp.exp(m_own - mx)
    l = jnp.sum(w_all * l_scr[...], axis=-1, keepdims=True) + w_own * l_own
    o = w_own * o_own
    for n in range(nblk):
        o = o + w_all[:, n:n + 1] * o_scr[n]
    o = o / l
    outs = []
    for h in range(MOBA_HEADS):
        kv = h // grp
        outs.append(o[h * t_new:(h + 1) * t_new, MOBA_HDIM * kv:MOBA_HDIM * (kv + 1)])
    o_ref[...] = jnp.concatenate(outs, axis=1)


def _moba_sample(u_b, cache, page_table, *, row0, t_new, pg=8):
    nb, npages = page_table.shape
    past = npages * PAGE_SIZE
    assert npages % pg == 0 and past % MOBA_BLOCK == 0 and (pg * PAGE_SIZE) % MOBA_BLOCK == 0
    nblk = past // MOBA_BLOCK
    assert nblk <= LANES and t_new == SUBLANES
    nrow = MOBA_HEADS * t_new
    cachev = jnp.transpose(cache, (0, 2, 3, 4, 1)).reshape(cache.shape[0], 2 * D_KV, PAGE_SIZE)
    rb0 = row0 // t_new
    grid_spec = pltpu.PrefetchScalarGridSpec(
        num_scalar_prefetch=1,
        grid=(nb,),
        in_specs=[pl.BlockSpec((t_new, D_Q), lambda b, pt: (rb0 + b, 0)),
                  pl.BlockSpec((t_new, 2 * D_KV), lambda b, pt: (rb0 + b, D_Q // (2 * D_KV))),
                  pl.BlockSpec(memory_space=pl.ANY)],
        out_specs=pl.BlockSpec((t_new, D_Q), lambda b, pt: (b, 0)),
        scratch_shapes=[pltpu.VMEM((2, pg, 2 * D_KV, PAGE_SIZE), F32), pltpu.SemaphoreType.DMA((2,)),
                        pltpu.VMEM((D_KV, LANES), F32),
                        pltpu.VMEM((nrow, LANES), F32), pltpu.VMEM((nrow, LANES), F32),
                        pltpu.VMEM((nblk, nrow, D_KV), F32)],
    )
    return pl.pallas_call(
        functools.partial(_moba_sample_kernel, pg=pg, nchunk=npages // pg),
        grid_spec=grid_spec,
        out_shape=jax.ShapeDtypeStruct((nb * t_new, D_Q), F32),
        compiler_params=_params(("arbitrary",), 40),
        name="moba_sample_attention",
    )(page_table, u_b, u_b, cachev)


def _odd_out_kernel(ocp_ref, ocs_ref, odp_ref, ods_ref, x_ref, w_ref, g_ref, b_ref, y_ref, *, npb):
    half = MLA_HEADS * MLA_VDIM

    def compute(oc_ref, od_ref):
        m = (_nn(oc_ref[...].astype(BF16), w_ref[0:half, :])
             + _nn(od_ref[...].astype(BF16), w_ref[half:half + D_Q, :]))
        y_ref[...] = _layer_norm_rows(DN_ALPHA * x_ref[...] + m, g_ref[...], b_ref[...])

    i = pl.program_id(0)

    @pl.when(i < npb)
    def _():
        compute(ocp_ref, odp_ref)

    @pl.when(i >= npb)
    def _():
        compute(ocs_ref, ods_ref)


def _odd_out(oc, od, x, w_bf16, g, b, *, n_prompt, tm=512):
    n = x.shape[0]
    assert n % tm == 0 and n_prompt % tm == 0
    npb = n_prompt // tm
    const = lambda a: pl.BlockSpec(a.shape, lambda i: (0, 0))
    in_specs = (_two_source_specs(oc[0].shape[1], tm, npb) + _two_source_specs(od[0].shape[1], tm, npb)
                + [pl.BlockSpec((tm, D_MODEL), lambda i: (i, 0)), const(w_bf16), const(g), const(b)])
    return pl.pallas_call(
        functools.partial(_odd_out_kernel, npb=npb),
        grid=(n // tm,),
        in_specs=in_specs,
        out_specs=pl.BlockSpec((tm, D_MODEL), lambda i: (i, 0)),
        out_shape=jax.ShapeDtypeStruct((n, D_MODEL), F32),
        compiler_params=_params(("arbitrary",), 40),
        name="odd_out_deepnorm",
    )(*oc, *od, x, w_bf16, g, b)


def _odd_layer(x, cache_mla, cache_moba, page_table, w_in, gq, w_uq, gkv, w_uk, w_uv, w_out, ln_g, ln_b,
               *, n_prompt, seq, t_new):
    n_tok = x.shape[0]
    n_sample = n_tok - n_prompt
    past = page_table.shape[1] * PAGE_SIZE
    w_in2, wqa, wqb, wk, wv, up, wvh = _odd_weights(w_in, w_uq, w_uk, w_uv)
    wa = MLA_KV_LORA + HEAD_W + MLA_Q_LORA
    u_a, u_b = _matmul(x, w_in2, splits=(wa, D_Q + 2 * D_KV))
    pos = jnp.concatenate([jnp.tile(jnp.arange(seq, dtype=jnp.int32), n_prompt // seq),
                           jnp.tile(past + jnp.arange(t_new, dtype=jnp.int32), n_sample // t_new)])
    cos_t, sin_t = _rope_tables(pos)
    lat, q, k, v = _odd_prep(u_a, cos_t, sin_t, gq.reshape(1, -1), gkv.reshape(1, -1), wqa, wqb, wk, wv)
    oc_p = _mla_prompt(q, k, v, n_prompt=n_prompt, seq=seq)
    q_abs = _absorb(q, up, row0=n_prompt, nrows=n_sample)
    o_lat = _mla_sample(q_abs, lat, cache_mla, page_table, row0=n_prompt, t_new=t_new)
    oc_s = _latent_up(o_lat, wvh)
    kmean = _block_means(u_b, n_prompt=n_prompt, seq=seq)
    od_p = _moba_prompt(u_b, kmean, n_prompt=n_prompt, seq=seq)
    od_s = _moba_sample(u_b, cache_moba, page_table, row0=n_prompt, t_new=t_new)
    w_out_b = w_out.astype(BF16)
    g2, b2 = ln_g.reshape(1, -1), ln_b.reshape(1, -1)
    y = _odd_out((oc_p, oc_s), (od_p, od_s), x, w_out_b, g2, b2, n_prompt=n_prompt)
    kv_new = u_b[:, D_Q:]
    return y, lat, kv_new


def kernel(x_prompt, x_sample, state_hgrn, cache_swa_w128, cache_swa_w512, cache_swa_w2048, cache_mla, cache_moba_kv, page_table, w_in_even, hgrn_lb_logits, hgrn_norm_g, w_out_even, w_in_odd, mla_q_norm_g, mla_w_uq, mla_kv_norm_g, mla_w_uk, mla_w_uv, w_out_odd, ln_mix_g, ln_mix_b, ln_ffn_g, ln_ffn_b, router_group_w, router_group_b, router_expert_w, router_expert_b, expert_w_gate, expert_w_up, expert_w_down):
    nbatch, seq, _ = x_prompt.shape
    nb_s, t_new, _ = x_sample.shape
    n_prompt = nbatch * seq
    x = jnp.concatenate([x_prompt.reshape(n_prompt, D_MODEL), x_sample.reshape(nb_s * t_new, D_MODEL)], axis=0)
    caches = [cache_swa_w128, cache_swa_w512, cache_swa_w2048]
    x, hgrn_p, hgrn_s, swa_p, swa_s = _even_layer(
        x, state_hgrn, caches, w_in_even, hgrn_lb_logits, hgrn_norm_g, w_out_even, ln_mix_g[0], ln_mix_b[0],
        n_prompt=n_prompt, seq=seq, t_new=t_new)

    def moe(x, layer):
        wr, br = _router_weights(router_group_w[layer], router_group_b[layer],
                                 router_expert_w[layer], router_expert_b[layer])
        return _hier_moe(x, wr, br, expert_w_gate[layer].astype(BF16), expert_w_up[layer].astype(BF16),
                         expert_w_down[layer].astype(BF16),
                         ln_ffn_g[layer].reshape(1, -1), ln_ffn_b[layer].reshape(1, -1))

    x = moe(x, 0)
    x, lat, kv_new = _odd_layer(
        x, cache_mla, cache_moba_kv, page_table, w_in_odd, mla_q_norm_g, mla_w_uq, mla_kv_norm_g, mla_w_uk,
        mla_w_uv, w_out_odd, ln_mix_g[1], ln_mix_b[1], n_prompt=n_prompt, seq=seq, t_new=t_new)
    x = moe(x, 1)
    kv_shape = (2, MOBA_KV_HEADS, MOBA_HDIM)
    return (x[:n_prompt].reshape(x_prompt.shape), x[n_prompt:].reshape(x_sample.shape),
            hgrn_p, hgrn_s, swa_p[0], swa_s[0], swa_p[1], swa_s[1], swa_p[2], swa_s[2],
            lat[:n_prompt].reshape(nbatch, seq, MLA_LAT), lat[n_prompt:].reshape(nb_s, t_new, MLA_LAT),
            kv_new[:n_prompt].reshape((nbatch, seq) + kv_shape), kv_new[n_prompt:].reshape((nb_s, t_new) + kv_shape))
```

```python
import functools
import math

import jax
import jax.numpy as jnp
from jax import lax
from jax.experimental import pallas as pl
from jax.experimental.pallas import tpu as pltpu

F32 = jnp.float32
BF16 = jnp.bfloat16

D_MODEL = 1024
DEPTH = 2
PAGE_SIZE = 128
HGRN_HEADS = 4
HGRN_KDIM = 128
HGRN_VDIM = 128
HGRN_CHUNK = 32
DIL_GROUPS = ((128, 1), (512, 4), (2048, 16))
DIL_HEADS = 4
DIL_HDIM = 64
MLA_HEADS = 8
MLA_Q_LORA = 384
MLA_KV_LORA = 256
MLA_NOPE = 64
MLA_ROPE = 32
MLA_VDIM = 64
ROPE_THETA = 10000.0
MOBA_HEADS = 8
MOBA_KV_HEADS = 4
MOBA_HDIM = 64
MOBA_BLOCK = 256
MOBA_TOPK = 3
MOE_GROUPS = 4
MOE_EXPERTS_PER_GROUP = 8
MOE_EXPERTS = MOE_GROUPS * MOE_EXPERTS_PER_GROUP
MOE_TOPK = 2
MOE_FF = 512
DN_ALPHA = (2 * DEPTH) ** 0.25
LN_EPS = 1e-5
RMS_EPS = 1e-6

A_QF = HGRN_HEADS * HGRN_KDIM
A_IV = HGRN_HEADS * HGRN_VDIM
B_W = DIL_HEADS * DIL_HDIM
EVEN_IN = 2 * A_QF + 2 * A_IV + 3 * len(DIL_GROUPS) * B_W
MLA_LAT = MLA_KV_LORA + MLA_ROPE
D_Q = MOBA_HEADS * MOBA_HDIM
D_KV = MOBA_KV_HEADS * MOBA_HDIM

LANES = 128
SUBLANES = 8
MIB = 2 ** 20

NEG_INF = float("-inf")


def _params(semantics, vmem_mib):
    return pltpu.CompilerParams(dimension_semantics=semantics, vmem_limit_bytes=vmem_mib * MIB)


def _nt(a, b):
    return lax.dot_general(a, b, (((1,), (1,)), ((), ())), preferred_element_type=F32)


def _tn(a, b):
    return lax.dot_general(a, b, (((0,), (0,)), ((), ())), preferred_element_type=F32)


def _nn(a, b):
    return jnp.dot(a, b, preferred_element_type=F32)


def _layer_norm_rows(z, g, b):
    zc = z - jnp.mean(z, axis=-1, keepdims=True)
    y = zc * lax.rsqrt(jnp.mean(zc * zc, axis=-1, keepdims=True) + LN_EPS)
    return y * g + b


def _matmul_kernel(x_ref, w_ref, *refs, tn, nsplit, residue_views):
    o_refs = refs[:nsplit]
    xb = x_ref[...].astype(BF16)
    base = 0
    for o_ref in o_refs:
        n = o_ref.shape[1]
        for c0 in range(0, n, tn):
            c1 = min(c0 + tn, n)
            o_ref[:, c0:c1] = _nn(xb, w_ref[:, base + c0:base + c1]).astype(o_ref.dtype)
        base += n
    if not residue_views:
        return
    stage = refs[-1]
    tm = x_ref.shape[0]
    for (c0, width, dil), r_ref in zip(residue_views, refs[nsplit:-1]):
        for cb in range(width // LANES):
            stage[...] = o_refs[0][:, c0 + cb * LANES:c0 + (cb + 1) * LANES]
            for r in range(dil):
                r_ref[:, r * width + cb * LANES:r * width + (cb + 1) * LANES] = (
                    stage[pl.ds(r, tm // dil, stride=dil), :])


def _matmul(x, w_bf16, *, splits=None, residue_views=(), tm=256, tn=512):
    m, k = x.shape
    n = w_bf16.shape[1]
    splits = splits or (n,)
    assert m % tm == 0 and sum(splits) == n and all(s % LANES == 0 for s in splits)
    assert all(tm % (SUBLANES * d) == 0 and wd % LANES == 0 for _, wd, d in residue_views)
    out_specs = [pl.BlockSpec((tm, s), lambda i: (i, 0)) for s in splits]
    out_shape = [jax.ShapeDtypeStruct((m, s), F32) for s in splits]
    for _, wd, d in residue_views:
        out_specs.append(pl.BlockSpec((tm // d, d * wd), lambda i: (i, 0)))
        out_shape.append(jax.ShapeDtypeStruct((m // d, d * wd), F32))
    outs = pl.pallas_call(
        functools.partial(_matmul_kernel, tn=tn, nsplit=len(splits), residue_views=tuple(residue_views)),
        grid=(m // tm,),
        in_specs=[pl.BlockSpec((tm, k), lambda i: (i, 0)),
                  pl.BlockSpec((k, n), lambda i: (0, 0))],
        out_specs=out_specs,
        out_shape=out_shape,
        scratch_shapes=[pltpu.VMEM((tm, LANES), F32)] if residue_views else [],
        compiler_params=_params(("parallel",), 48),
        name="dense_projection",
    )(x, w_bf16)
    return outs[0] if len(outs) == 1 else outs


def _chunk_cumsum(x, c):
    row = lax.broadcasted_iota(jnp.int32, x.shape, 0) & (c - 1)
    y = x
    s = 1
    while s < c:
        y = y + jnp.where(row >= s, pltpu.roll(y, s, 0), 0.0)
        s *= 2
    return y


def _hgrn_kernel(q_ref, f_ref, i_ref, g_ref, lb_ref, ng_ref, s0_ref, o_ref, sfin_ref, st_scr,
                 *, c, sb, nsb):
    t = pl.program_id(1)

    @pl.when(t == 0)
    def _():
        for h in range(HGRN_HEADS):
            st_scr[h] = s0_ref[0, h].T

    nchunk = sb // c
    row = lax.broadcasted_iota(jnp.int32, (sb, sb), 0)
    col = lax.broadcasted_iota(jnp.int32, (sb, sb), 1)
    causal = (col <= row) & ((row // c) == (col // c))
    ng = ng_ref[...]

    def sub(j, carry):
        rows = pl.ds(pl.multiple_of(j * sb, sb), sb)
        for h in range(HGRN_HEADS):
            cs = slice(HGRN_KDIM * h, HGRN_KDIM * (h + 1))
            q = q_ref[rows, cs]
            fp = f_ref[rows, cs]
            v = i_ref[rows, cs]
            g = g_ref[rows, cs]
            lb = lb_ref[:, cs]
            lf = jnp.log(lb + (1.0 - lb) * jax.nn.sigmoid(fp))
            k = (1.0 - lb) * jax.nn.sigmoid(-fp)
            cum = _chunk_cumsum(lf, c)
            last = jnp.concatenate(
                [jnp.broadcast_to(cum[ci * c + c - 1:ci * c + c, :], (c, HGRN_KDIM)) for ci in range(nchunk)],
                axis=0)
            qd = q * jnp.exp(cum)
            kd = k * jnp.exp(-cum)
            kl = k * jnp.exp(last - cum)
            att = jnp.where(causal, _nt(qd, kd), 0.0)
            intra = _nn(att, v)
            st = st_scr[h]
            outs = []
            for ci in range(nchunk):
                r = slice(ci * c, (ci + 1) * c)
                outs.append(intra[r] + _nt(qd[r], st))
                st = st * jnp.exp(last[ci * c:ci * c + 1, :]) + _tn(v[r], kl[r])
            st_scr[h] = st
            o = jnp.concatenate(outs, axis=0) if nchunk > 1 else outs[0]
            y = o * lax.rsqrt(jnp.mean(o * o, axis=-1, keepdims=True) + RMS_EPS) * ng
            o_ref[rows, cs] = (y * (g * jax.nn.sigmoid(g))).astype(o_ref.dtype)
        return carry

    lax.fori_loop(0, nsb, sub, 0)

    @pl.when(t == pl.num_programs(1) - 1)
    def _():
        for h in range(HGRN_HEADS):
            sfin_ref[0, h] = st_scr[h].T


def _hgrn(u, lb, norm_g, s0, *, row0, seq, tb, c):
    nb = s0.shape[0]
    nt = seq // tb
    sb = min(tb, LANES)
    assert seq % tb == 0 and tb % sb == 0 and sb % c == 0 and row0 % tb == 0
    rb0 = row0 // tb
    in_specs = [pl.BlockSpec((tb, A_QF), lambda b, t, j=j: (rb0 + b * nt + t, j)) for j in range(4)]
    in_specs += [pl.BlockSpec((1, A_QF), lambda b, t: (0, 0)),
                 pl.BlockSpec((1, HGRN_VDIM), lambda b, t: (0, 0)),
                 pl.BlockSpec((1, HGRN_HEADS, HGRN_KDIM, HGRN_VDIM), lambda b, t: (b, 0, 0, 0))]
    return pl.pallas_call(
        functools.partial(_hgrn_kernel, c=c, sb=sb, nsb=tb // sb),
        grid=(nb, nt),
        in_specs=in_specs,
        out_specs=[pl.BlockSpec((tb, A_IV), lambda b, t: (b * nt + t, 0)),
                   pl.BlockSpec((1, HGRN_HEADS, HGRN_KDIM, HGRN_VDIM), lambda b, t: (b, 0, 0, 0))],
        out_shape=[jax.ShapeDtypeStruct((nb * seq, A_IV), F32),
                   jax.ShapeDtypeStruct(s0.shape, F32)],
        scratch_shapes=[pltpu.VMEM((HGRN_HEADS, HGRN_VDIM, HGRN_KDIM), F32)],
        compiler_params=_params(("parallel", "arbitrary"), 48),
        name="hgrn2_scan",
    )(u, u, u, u, lb, norm_g, s0)


def _dil_prompt_kernel(q_ref, kc_ref, vc_ref, kp_ref, vp_ref, o_ref, l_ref):
    w = q_ref.shape[0]
    n = pl.program_id(1)
    a = lax.broadcasted_iota(jnp.int32, (w, 2 * w), 0)
    cc = lax.broadcasted_iota(jnp.int32, (w, 2 * w), 1)
    lo = a + jnp.where(n > 0, 0, 2 * w)
    valid = ((cc >= w) & ((cc - w) <= a)) | ((cc < w) & (cc >= lo))
    scale = DIL_HDIM ** -0.5
    outs, lses = [], []
    for h in range(DIL_HEADS):
        cs = slice(DIL_HDIM * h, DIL_HDIM * (h + 1))
        q = q_ref[:, cs].astype(BF16)
        k = jnp.concatenate([kp_ref[:, cs], kc_ref[:, cs]], axis=0).astype(BF16)
        v = jnp.concatenate([vp_ref[:, cs], vc_ref[:, cs]], axis=0).astype(BF16)
        s = jnp.where(valid, _nt(q, k) * scale, NEG_INF)
        m = jnp.max(s, axis=-1, keepdims=True)
        e = jnp.exp(s - m)
        l = jnp.sum(e, axis=-1, keepdims=True)
        outs.append(_nn(e.astype(BF16), v) / l)
        lses.append(jnp.broadcast_to(m + jnp.log(l), (w, DIL_HDIM)))
    o_ref[...] = jnp.concatenate(outs, axis=1)
    l_ref[...] = jnp.concatenate(lses, axis=1)


def _dil_prompt(uv, grp, *, n_prompt, seq, ncb, cb0):
    win, dil = DIL_GROUPS[grp]
    w = win // dil
    unit = dil * w
    assert seq % unit == 0 and w == LANES and uv.shape[1] == dil * ncb * B_W
    nbatch = n_prompt // seq
    nblk = seq // unit

    def spec(cb, prev):
        def imap(b, n, r):
            nn_ = jnp.maximum(n - 1, 0) if prev else n
            return (b * nblk + nn_, r * ncb + cb)
        return pl.BlockSpec((w, B_W), imap)

    out_spec = pl.BlockSpec((w, B_W), lambda b, n, r: (b * nblk + n, r))
    o, lse = pl.pallas_call(
        _dil_prompt_kernel,
        grid=(nbatch, nblk, dil),
        in_specs=[spec(cb0, False), spec(cb0 + 1, False), spec(cb0 + 2, False),
                  spec(cb0 + 1, True), spec(cb0 + 2, True)],
        out_specs=[out_spec, out_spec],
        out_shape=[jax.ShapeDtypeStruct((n_prompt // dil, dil * B_W), F32)] * 2,
        compiler_params=_params(("parallel", "arbitrary", "arbitrary"), 32),
        name=f"dilated_prompt_w{win}",
    )(uv, uv, uv, uv, uv)
    return o.reshape(n_prompt, B_W), lse.reshape(n_prompt, B_W)


def _dil_sample_kernel(q_ref, k_ref, v_ref, buf_ref, nbuf_ref, o_ref, l_ref, *, dil):
    lb = buf_ref.shape[2]
    t_new = q_ref.shape[0]
    nrow = DIL_HEADS * t_new
    q = q_ref[...]
    lane_head = lax.broadcasted_iota(jnp.int32, (t_new, B_W), 1) // DIL_HDIM
    qbd = jnp.concatenate([jnp.where(lane_head == h, q, 0.0) for h in range(DIL_HEADS)], axis=0).astype(BF16)
    scale = DIL_HDIM ** -0.5

    old = buf_ref[0]
    k_old = old[0:B_W].astype(BF16)
    v_old = old[B_W:2 * B_W].astype(BF16)
    k_new = k_ref[...].astype(BF16)
    v_new = v_ref[...].astype(BF16)

    tq = lax.broadcasted_iota(jnp.int32, (nrow, lb), 0) % t_new
    d_old = lax.broadcasted_iota(jnp.int32, (nrow, lb), 1) - tq
    valid_m = (d_old >= 0) & ((d_old & (dil - 1)) == 0)
    th = lax.broadcasted_iota(jnp.int32, (nrow, t_new), 0) % t_new
    d_new = th - lax.broadcasted_iota(jnp.int32, (nrow, t_new), 1)
    valid_h = (d_new >= 0) & ((d_new & (dil - 1)) == 0)

    s_m = jnp.where(valid_m, _nn(qbd, k_old) * scale, NEG_INF)
    s_h = jnp.where(valid_h, _nt(qbd, k_new) * scale, NEG_INF)
    m = jnp.maximum(jnp.max(s_m, axis=-1, keepdims=True), jnp.max(s_h, axis=-1, keepdims=True))
    e_m = jnp.exp(s_m - m)
    e_h = jnp.exp(s_h - m)
    l = jnp.sum(e_m, axis=-1, keepdims=True) + jnp.sum(e_h, axis=-1, keepdims=True)
    o_all = (_nt(e_m.astype(BF16), v_old) + _nn(e_h.astype(BF16), v_new)) / l

    nbuf_ref[0] = pltpu.roll(old, lb - t_new, 1)
    kv = jnp.concatenate([k_ref[...], v_ref[...]], axis=1)
    kv_t = jnp.concatenate([kv, jnp.zeros((LANES - t_new, 2 * B_W), F32)], axis=0).T
    lane = lax.broadcasted_iota(jnp.int32, (2 * B_W, LANES), 1)
    tail = jnp.where(lane >= LANES - t_new, pltpu.roll(kv_t, LANES - t_new, 1), nbuf_ref[0, :, lb - LANES:lb])
    nbuf_ref[0, :, lb - LANES:lb] = tail
    lse_all = jnp.broadcast_to(m + jnp.log(l), (nrow, B_W))
    o = jnp.zeros((t_new, B_W), F32)
    lse = jnp.zeros((t_new, B_W), F32)
    for h in range(DIL_HEADS):
        r = slice(h * t_new, (h + 1) * t_new)
        o = jnp.where(lane_head == h, o_all[r], o)
        lse = jnp.where(lane_head == h, lse_all[r], lse)
    o_ref[...] = o
    l_ref[...] = lse


def _dil_sample(u, buf, grp, *, n_prompt, t_new):
    win, dil = DIL_GROUPS[grp]
    nb, lb = buf.shape[0], buf.shape[1]
    assert lb == win and t_new == SUBLANES and n_prompt % t_new == 0
    cb0 = (2 * A_QF + 2 * A_IV) // B_W + 3 * grp
    rb0 = n_prompt // t_new
    bufv = jnp.transpose(buf, (0, 2, 3, 4, 1)).reshape(nb, 2 * B_W, lb)

    def uspec(cb):
        return pl.BlockSpec((t_new, B_W), lambda b: (rb0 + b, cb))

    tok_spec = pl.BlockSpec((t_new, B_W), lambda b: (b, 0))
    nbuf, o, lse = pl.pallas_call(
        functools.partial(_dil_sample_kernel, dil=dil),
        grid=(nb,),
        in_specs=[uspec(cb0), uspec(cb0 + 1), uspec(cb0 + 2),
                  pl.BlockSpec((1, 2 * B_W, lb), lambda b: (b, 0, 0))],
        out_specs=[pl.BlockSpec((1, 2 * B_W, lb), lambda b: (b, 0, 0)), tok_spec, tok_spec],
        out_shape=[jax.ShapeDtypeStruct(bufv.shape, F32),
                   jax.ShapeDtypeStruct((nb * t_new, B_W), F32),
                   jax.ShapeDtypeStruct((nb * t_new, B_W), F32)],
        compiler_params=_params(("parallel",), 48),
        name=f"dilated_sample_w{win}",
    )(u, u, u, bufv)
    nbuf = jnp.transpose(nbuf.reshape(nb, 2, DIL_HEADS, DIL_HDIM, lb), (0, 4, 1, 2, 3))
    return nbuf, o, lse


def _two_source_specs(width, tm, npb):
    return [pl.BlockSpec((tm, width), lambda i: (jnp.minimum(i, npb - 1), 0)),
            pl.BlockSpec((tm, width), lambda i: (jnp.maximum(i - npb, 0), 0))]


def _even_out_kernel(*refs, npb):
    srcs = refs[:14]
    x_ref, w_ref, g_ref, b_ref, y_ref = refs[14:]

    def compute(ga_ref, os_, ls_):
        lses = [l[...] for l in ls_]
        mx = jnp.maximum(jnp.maximum(lses[0], lses[1]), lses[2])
        es = [jnp.exp(l - mx) for l in lses]
        tot = es[0] + es[1] + es[2]
        ob = (es[0] / tot) * os_[0][...] + (es[1] / tot) * os_[1][...] + (es[2] / tot) * os_[2][...]
        m = _nn(ga_ref[...].astype(BF16), w_ref[0:A_IV, :]) + _nn(ob.astype(BF16), w_ref[A_IV:A_IV + B_W, :])
        y_ref[...] = _layer_norm_rows(DN_ALPHA * x_ref[...] + m, g_ref[...], b_ref[...])

    i = pl.program_id(0)
    for src, cond in ((0, i < npb), (1, i >= npb)):
        @pl.when(cond)
        def _(src=src):
            compute(srcs[src], srcs[2 + src:8:2], srcs[8 + src:14:2])


def _even_out(gated, os_, lses, x, w_bf16, g, b, *, n_prompt, tm=512):
    n = x.shape[0]
    assert n % tm == 0 and n_prompt % tm == 0
    npb = n_prompt // tm
    const = lambda shape: pl.BlockSpec(shape, lambda i: (0, 0))
    in_specs = _two_source_specs(A_IV, tm, npb)
    args = list(gated)
    for pair in list(os_) + list(lses):
        in_specs += _two_source_specs(B_W, tm, npb)
        args += list(pair)
    in_specs += [pl.BlockSpec((tm, D_MODEL), lambda i: (i, 0)), const(w_bf16.shape),
                 const((1, D_MODEL)), const((1, D_MODEL))]
    return pl.pallas_call(
        functools.partial(_even_out_kernel, npb=npb),
        grid=(n // tm,),
        in_specs=in_specs,
        out_specs=pl.BlockSpec((tm, D_MODEL), lambda i: (i, 0)),
        out_shape=jax.ShapeDtypeStruct((n, D_MODEL), F32),
        compiler_params=_params(("arbitrary",), 40),
        name="even_out_deepnorm",
    )(*args, x, w_bf16, g, b)


def _router_kernel(x_ref, w_ref, b_ref, e_ref, p_ref):
    logits = _nn(x_ref[...].astype(BF16), w_ref[...]) + b_ref[...]
    tm = logits.shape[0]
    lane = lax.broadcasted_iota(jnp.int32, logits.shape, 1)
    big = jnp.int32(LANES)
    is_g = lane < MOE_GROUPS
    gl = jnp.where(is_g, logits, NEG_INF)
    gmax = jnp.max(gl, axis=-1, keepdims=True)
    gsel = jnp.min(jnp.where(is_g & (gl == gmax), lane, big), axis=-1, keepdims=True)
    gw = 1.0 / jnp.sum(jnp.where(is_g, jnp.exp(gl - gmax), 0.0), axis=-1, keepdims=True)
    e_id = lane - MOE_GROUPS
    in_grp = (e_id >= gsel * MOE_EXPERTS_PER_GROUP) & (e_id < (gsel + 1) * MOE_EXPERTS_PER_GROUP)
    el = jnp.where(in_grp, logits, NEG_INF)
    v1 = jnp.max(el, axis=-1, keepdims=True)
    i1 = jnp.min(jnp.where(in_grp & (el == v1), e_id, big), axis=-1, keepdims=True)
    el2 = jnp.where(e_id == i1, NEG_INF, el)
    v2 = jnp.max(el2, axis=-1, keepdims=True)
    i2 = jnp.min(jnp.where(in_grp & (e_id != i1) & (el2 == v2), e_id, big), axis=-1, keepdims=True)
    e2 = jnp.exp(v2 - v1)
    den = 1.0 + e2
    w1 = gw * (1.0 / den)
    w2 = gw * (e2 / den)
    e_ref[...] = jnp.where(lane == 0, i1, jnp.where(lane == 1, i2, 0))
    p_ref[...] = jnp.where(lane == 0, w1, jnp.where(lane == 1, w2, 0.0))
    del tm


def _router(x, w_bf16, bias, *, tm=512):
    n = x.shape[0]
    row = lambda dt: pl.BlockSpec((tm, LANES), lambda i: (i, 0))
    return pl.pallas_call(
        _router_kernel,
        grid=(n // tm,),
        in_specs=[pl.BlockSpec((tm, D_MODEL), lambda i: (i, 0)),
                  pl.BlockSpec((D_MODEL, LANES), lambda i: (0, 0)),
                  pl.BlockSpec((1, LANES), lambda i: (0, 0))],
        out_specs=[row(jnp.int32), row(F32)],
        out_shape=[jax.ShapeDtypeStruct((n, LANES), jnp.int32), jax.ShapeDtypeStruct((n, LANES), F32)],
        compiler_params=_params(("parallel",), 32),
        name="moe_router",
    )(x, w_bf16, bias)


def _expert_kernel(blk_e_ref, src_ref, dst_ref, x_hbm, wg_ref, wu_ref, wd_ref, y_hbm,
                   xg, yo, gsem, ssem, *, rows):
    del blk_e_ref
    i = pl.program_id(0)
    nsteps = pl.num_programs(0)
    slot = i % 2

    def gather_start(step, sl):
        base = step * rows
        for r in range(rows):
            pltpu.make_async_copy(x_hbm.at[pl.ds(src_ref[base + r], 1)], xg.at[sl, pl.ds(r, 1)],
                                  gsem.at[sl]).start()

    def scatter_start(step, sl):
        base = step * rows
        for r in range(rows):
            pltpu.make_async_copy(yo.at[sl, pl.ds(r, 1)], y_hbm.at[pl.ds(dst_ref[base + r], 1)],
                                  ssem.at[sl]).start()

    def gather_wait(sl):
        pltpu.make_async_copy(x_hbm.at[pl.ds(0, rows)], xg.at[sl], gsem.at[sl]).wait()

    def scatter_wait(sl):
        pltpu.make_async_copy(yo.at[sl], y_hbm.at[pl.ds(0, rows)], ssem.at[sl]).wait()

    @pl.when(i == 0)
    def _():
        gather_start(0, 0)
        spare = y_hbm.shape[0] - 2 * rows
        for sl in range(2):
            yo[sl] = jnp.zeros((rows, D_MODEL), F32)
            fill = pltpu.make_async_copy(yo.at[sl], y_hbm.at[pl.ds(spare + sl * rows, rows)], ssem.at[sl])
            fill.start()
            fill.wait()

    @pl.when(i + 1 < nsteps)
    def _():
        gather_start(i + 1, 1 - slot)

    gather_wait(slot)

    @pl.when(i >= 2)
    def _():
        scatter_wait(slot)

    xb = xg[slot].astype(BF16)
    hdn = _nn(xb, wg_ref[0])
    hdn = (hdn * jax.nn.sigmoid(hdn)) * _nn(xb, wu_ref[0])
    yo[slot] = _nn(hdn.astype(BF16), wd_ref[0])
    scatter_start(i, slot)

    @pl.when(i == nsteps - 1)
    def _():
        scatter_wait(slot)

        @pl.when(nsteps >= 2)
        def _():
            scatter_wait(1 - slot)


def _experts(x, blk_e, src, dst, wg, wu, wd, *, rows):
    n = x.shape[0]
    nblk = blk_e.shape[0]
    wspec = lambda shape: pl.BlockSpec((1,) + shape, lambda i, be, s, d: (be[i], 0, 0))
    grid_spec = pltpu.PrefetchScalarGridSpec(
        num_scalar_prefetch=3,
        grid=(nblk,),
        in_specs=[pl.BlockSpec(memory_space=pl.ANY),
                  wspec((D_MODEL, MOE_FF)), wspec((D_MODEL, MOE_FF)), wspec((MOE_FF, D_MODEL))],
        out_specs=pl.BlockSpec(memory_space=pl.ANY),
        scratch_shapes=[pltpu.VMEM((2, rows, D_MODEL), F32), pltpu.VMEM((2, rows, D_MODEL), F32),
                        pltpu.SemaphoreType.DMA((2,)), pltpu.SemaphoreType.DMA((2,))],
    )
    return pl.pallas_call(
        functools.partial(_expert_kernel, rows=rows),
        grid_spec=grid_spec,
        out_shape=jax.ShapeDtypeStruct((MOE_TOPK * n + 2 * rows, D_MODEL), F32),
        compiler_params=_params(("arbitrary",), 40),
        name="moe_experts",
    )(blk_e, src, dst, x, wg, wu, wd)


def _moe_combine_kernel(y0_ref, y1_ref, p_ref, x_ref, g_ref, b_ref, o_ref):
    p = p_ref[...]
    f = y0_ref[...] * p[:, 0:1] + y1_ref[...] * p[:, 1:2]
    o_ref[...] = _layer_norm_rows(DN_ALPHA * x_ref[...] + f, g_ref[...], b_ref[...])


def _moe_combine(y2, p, x, g, b, *, tm=512):
    n = x.shape[0]
    assert n % tm == 0
    return pl.pallas_call(
        _moe_combine_kernel,
        grid=(n // tm,),
        in_specs=[pl.BlockSpec((tm, D_MODEL), lambda i: (i, 0)),
                  pl.BlockSpec((tm, D_MODEL), lambda i: (n // tm + i, 0)),
                  pl.BlockSpec((tm, LANES), lambda i: (i, 0)),
                  pl.BlockSpec((tm, D_MODEL), lambda i: (i, 0)),
                  pl.BlockSpec((1, D_MODEL), lambda i: (0, 0)),
                  pl.BlockSpec((1, D_MODEL), lambda i: (0, 0))],
        out_specs=pl.BlockSpec((tm, D_MODEL), lambda i: (i, 0)),
        out_shape=jax.ShapeDtypeStruct((n, D_MODEL), F32),
        compiler_params=_params(("parallel",), 40),
        name="moe_combine_deepnorm",
    )(y2, y2, p, x, g, b)


def _dispatch_plan(eidx, *, rows):
    n = eidx.shape[0]
    na = n * MOE_TOPK
    flat_e = eidx.T.reshape(-1)
    order = jnp.argsort(flat_e).astype(jnp.int32)
    se = flat_e[order]
    counts = jnp.bincount(flat_e, length=MOE_EXPERTS).astype(jnp.int32)
    padded = (counts + rows - 1) // rows * rows
    pad_end = jnp.cumsum(padded)
    pad_start = pad_end - padded
    start = jnp.cumsum(counts) - counts
    dest = pad_start[se] + jnp.arange(na, dtype=jnp.int32) - start[se]
    nblk = -(-(na + MOE_EXPERTS * (rows - 1)) // rows)
    dst = jnp.full((nblk * rows,), -1, jnp.int32).at[dest].set(order)
    pad = dst < 0
    src = jnp.where(pad, 0, dst % n)
    dst = jnp.where(pad, na + jnp.arange(nblk * rows, dtype=jnp.int32) % (2 * rows), dst)
    blk_e = jnp.minimum(jnp.searchsorted(pad_end, jnp.arange(nblk, dtype=jnp.int32) * rows, side="right"),
                        MOE_EXPERTS - 1).astype(jnp.int32)
    return blk_e, src, dst


def _hier_moe(x, wr_bf16, br, wg, wu, wd, ln_g, ln_b, *, rows=256):
    e_pad, p_pad = _router(x, wr_bf16, br)
    blk_e, src, dst = _dispatch_plan(e_pad[:, :MOE_TOPK], rows=rows)
    y2 = _experts(x, blk_e, src, dst, wg, wu, wd, rows=rows)
    return _moe_combine(y2, p_pad, x, ln_g, ln_b)


def _router_weights(wg, bg, we, be):
    w = jnp.zeros((D_MODEL, LANES), F32).at[:, :MOE_GROUPS].set(wg).at[:, MOE_GROUPS:MOE_GROUPS + MOE_EXPERTS].set(we)
    b = jnp.zeros((1, LANES), F32).at[0, :MOE_GROUPS].set(bg).at[0, MOE_GROUPS:MOE_GROUPS + MOE_EXPERTS].set(be)
    return w.astype(BF16), b


def _even_layer(x, state_hgrn, caches, w_in, lb_logits, norm_g, w_out, ln_g, ln_b, *, n_prompt, seq, t_new):
    n_tok = x.shape[0]
    nbatch = n_prompt // seq
    cb0 = 2 * A_QF + 2 * A_IV
    views = [(cb0 + 3 * B_W * grp, 3 * B_W, dil) for grp, (_, dil) in enumerate(DIL_GROUPS) if dil > 1]
    u, *u_res = _matmul(x, w_in.astype(BF16), residue_views=views)
    lb = jnp.cumsum(jax.nn.softmax(lb_logits.astype(F32), axis=0), axis=0)[0].reshape(1, A_QF)
    ng = norm_g.reshape(1, HGRN_VDIM)
    zeros = jnp.zeros((nbatch, HGRN_HEADS, HGRN_KDIM, HGRN_VDIM), F32)
    gated_p, hgrn_p = _hgrn(u, lb, ng, zeros, row0=0, seq=seq, tb=512, c=HGRN_CHUNK)
    gated_s, hgrn_s = _hgrn(u, lb, ng, state_hgrn, row0=n_prompt, seq=t_new, tb=t_new, c=t_new)
    os_, lses, swa_p, swa_s = [], [], [], []
    u_res = iter(u_res)
    for grp, (win, dil) in enumerate(DIL_GROUPS):
        if dil > 1:
            o_p, lse_p = _dil_prompt(next(u_res), grp, n_prompt=n_prompt, seq=seq, ncb=3, cb0=0)
        else:
            o_p, lse_p = _dil_prompt(u, grp, n_prompt=n_prompt, seq=seq, ncb=EVEN_IN // B_W,
                                     cb0=cb0 // B_W + 3 * grp)
        nbuf, o_s, lse_s = _dil_sample(u, caches[grp], grp, n_prompt=n_prompt, t_new=t_new)
        os_.append((o_p, o_s))
        lses.append((lse_p, lse_s))
        swa_s.append(nbuf)
        kv = u[:n_prompt, cb0 + 3 * B_W * grp + B_W:cb0 + 3 * B_W * (grp + 1)]
        kv = kv.reshape(nbatch, seq, 2, DIL_HEADS, DIL_HDIM)
        swa_p.append(kv[:, seq - min(win, seq):])
    y = _even_out((gated_p, gated_s), os_, lses, x, w_out.astype(BF16), ln_g.reshape(1, -1), ln_b.reshape(1, -1),
                  n_prompt=n_prompt)
    return y, hgrn_p, hgrn_s, swa_p, swa_s


HEAD_W = LANES
ROPE_LO = MLA_NOPE
ROPE_MID = MLA_NOPE + MLA_ROPE // 2
ROPE_HI = MLA_NOPE + MLA_ROPE


def _rms_rows(x, g):
    return x * lax.rsqrt(jnp.mean(x * x, axis=-1, keepdims=True) + RMS_EPS) * g


def _odd_prep_kernel(u_ref, cos_ref, sin_ref, gq_ref, gkv_ref, wqa_ref, wqb_ref, wk_ref, wv_ref,
                     lat_ref, q_ref, k_ref, v_ref):
    ckv = u_ref[:, 0:MLA_KV_LORA]
    krb = u_ref[:, MLA_KV_LORA:MLA_KV_LORA + HEAD_W]
    cq = u_ref[:, MLA_KV_LORA + HEAD_W:MLA_KV_LORA + HEAD_W + MLA_Q_LORA]
    cos = cos_ref[...]
    sin = sin_ref[...]
    cos_h = jnp.concatenate([cos] * MLA_HEADS, axis=1)
    sin_h = jnp.concatenate([sin] * MLA_HEADS, axis=1)
    cqn = _rms_rows(cq, gq_ref[...]).astype(BF16)
    q_ref[...] = (_nn(cqn, wqa_ref[...]) * cos_h + _nn(cqn, wqb_ref[...]) * sin_h).astype(BF16)
    c = _rms_rows(ckv, gkv_ref[...])
    lane = lax.broadcasted_iota(jnp.int32, krb.shape, 1)
    swapped = jnp.where(lane < ROPE_MID, pltpu.roll(krb, HEAD_W - MLA_ROPE // 2, 1), pltpu.roll(krb, MLA_ROPE // 2, 1))
    rot = krb * cos + swapped * sin
    lat_ref[:, 0:MLA_KV_LORA] = c
    lat_ref[:, MLA_KV_LORA:MLA_LAT] = rot[:, ROPE_LO:ROPE_HI]
    cb = c.astype(BF16)
    k_ref[...] = (_nn(cb, wk_ref[...]) + jnp.concatenate([rot] * MLA_HEADS, axis=1)).astype(BF16)
    v_ref[...] = _nn(cb, wv_ref[...]).astype(BF16)


def _odd_prep(u_a, cos_t, sin_t, gq, gkv, wqa, wqb, wk, wv, *, tm=256):
    n = u_a.shape[0]
    row = lambda w_: pl.BlockSpec((tm, w_), lambda i: (i, 0))
    const = lambda a: pl.BlockSpec(a.shape, lambda i: (0, 0))
    hw = MLA_HEADS * HEAD_W
    return pl.pallas_call(
        _odd_prep_kernel,
        grid=(n // tm,),
        in_specs=[row(u_a.shape[1]), row(HEAD_W), row(HEAD_W), const(gq), const(gkv),
                  const(wqa), const(wqb), const(wk), const(wv)],
        out_specs=[row(MLA_LAT), row(hw), row(hw), row(MLA_HEADS * MLA_VDIM)],
        out_shape=[jax.ShapeDtypeStruct((n, MLA_LAT), F32), jax.ShapeDtypeStruct((n, hw), BF16),
                   jax.ShapeDtypeStruct((n, hw), BF16), jax.ShapeDtypeStruct((n, MLA_HEADS * MLA_VDIM), BF16)],
        compiler_params=_params(("parallel",), 40),
        name="odd_prep",
    )(u_a, cos_t, sin_t, gq, gkv, wqa, wqb, wk, wv)


def _odd_weights(w_in, w_uq, w_uk, w_uv):
    o = 0
    cq_w = w_in[:, o:o + MLA_Q_LORA]; o += MLA_Q_LORA
    ckv_w = w_in[:, o:o + MLA_KV_LORA]; o += MLA_KV_LORA
    kr_w = w_in[:, o:o + MLA_ROPE]; o += MLA_ROPE
    rest = w_in[:, o:]
    zeros = lambda w_: jnp.zeros((D_MODEL, w_), w_in.dtype)
    w_in2 = jnp.concatenate([ckv_w, zeros(ROPE_LO), kr_w, zeros(HEAD_W - ROPE_HI), cq_w, rest], axis=1)

    half = MLA_ROPE // 2
    nope = w_uq[:, :, :MLA_NOPE]
    x1 = w_uq[:, :, MLA_NOPE:MLA_NOPE + half]
    x2 = w_uq[:, :, MLA_NOPE + half:]
    pad = jnp.zeros((MLA_Q_LORA, MLA_HEADS, HEAD_W - ROPE_HI), w_uq.dtype)
    wqa = jnp.concatenate([nope, x1, x2, pad], axis=2).reshape(MLA_Q_LORA, MLA_HEADS * HEAD_W)
    wqb = jnp.concatenate([jnp.zeros_like(nope), x2, x1, pad], axis=2).reshape(MLA_Q_LORA, MLA_HEADS * HEAD_W)
    kpad = jnp.zeros((MLA_KV_LORA, MLA_HEADS, HEAD_W - MLA_NOPE), w_uk.dtype)
    wk = jnp.concatenate([w_uk, kpad], axis=2).reshape(MLA_KV_LORA, MLA_HEADS * HEAD_W)
    wv = w_uv.reshape(MLA_KV_LORA, MLA_HEADS * MLA_VDIM)
    up = jnp.zeros((MLA_HEADS, HEAD_W, MLA_LAT), F32)
    up = up.at[:, :MLA_NOPE, :MLA_KV_LORA].set(jnp.transpose(w_uk, (1, 2, 0)))
    up = up.at[:, ROPE_LO:ROPE_HI, MLA_KV_LORA:].set(jnp.broadcast_to(jnp.eye(MLA_ROPE, dtype=F32), (MLA_HEADS, MLA_ROPE, MLA_ROPE)))
    wvh = jnp.transpose(w_uv, (1, 0, 2))
    return (w_in2.astype(BF16), wqa.astype(BF16), wqb.astype(BF16), wk.astype(BF16), wv.astype(BF16),
            up.astype(BF16), wvh.astype(BF16))


def _rope_tables(pos):
    half = MLA_ROPE // 2
    inv = ROPE_THETA ** (-jnp.arange(half, dtype=F32) / half)
    ang = pos.astype(F32)[:, None] * inv[None, :]
    cos, sin = jnp.cos(ang), jnp.sin(ang)
    n = pos.shape[0]
    cos_t = jnp.concatenate([jnp.ones((n, ROPE_LO), F32), cos, cos, jnp.zeros((n, HEAD_W - ROPE_HI), F32)], axis=1)
    sin_t = jnp.concatenate([jnp.zeros((n, ROPE_LO), F32), -sin, sin, jnp.zeros((n, HEAD_W - ROPE_HI), F32)], axis=1)
    return cos_t, sin_t


BIG_NEG = -1e30


def _causal_tiles(nq):
    qs = [i for i in range(nq) for _ in range(i + 1)]
    ks = [j for i in range(nq) for j in range(i + 1)]
    return jnp.asarray(qs, jnp.int32), jnp.asarray(ks, jnp.int32)


def _mla_flash_kernel(qt_ref, kt_ref, q_ref, k_ref, v_ref, o_ref, m_scr, l_scr, acc_scr):
    step = pl.program_id(2)
    qi = qt_ref[step]
    ki = kt_ref[step]
    tq, tk = q_ref.shape[0], k_ref.shape[0]
    scale = (MLA_NOPE + MLA_ROPE) ** -0.5

    @pl.when(ki == 0)
    def _():
        m_scr[...] = jnp.full(m_scr.shape, BIG_NEG, F32)
        l_scr[...] = jnp.zeros(l_scr.shape, F32)
        acc_scr[...] = jnp.zeros(acc_scr.shape, F32)

    nfold = tk // LANES

    def update(masked):
        v = v_ref[...]
        if masked:
            row = lax.broadcasted_iota(jnp.int32, (tq, tk), 0)
            col = lax.broadcasted_iota(jnp.int32, (tq, tk), 1)
            keep = col <= row
        for hh in range(2):
            cs = slice(HEAD_W * hh, HEAD_W * (hh + 1))
            s = _nt(q_ref[:, cs], k_ref[:, cs]) * scale
            if masked:
                s = jnp.where(keep, s, BIG_NEG)
            s_fold = s[:, 0:LANES]
            for i in range(1, nfold):
                s_fold = jnp.maximum(s_fold, s[:, i * LANES:(i + 1) * LANES])
            m_old = m_scr[hh]
            m_new = jnp.maximum(m_old, jnp.max(s_fold, axis=-1, keepdims=True))
            alpha = jnp.exp(m_old - m_new)
            p = jnp.exp(s - jnp.concatenate([m_new] * nfold, axis=1))
            p_fold = p[:, 0:LANES]
            for i in range(1, nfold):
                p_fold = p_fold + p[:, i * LANES:(i + 1) * LANES]
            l_scr[hh] = alpha * l_scr[hh] + p_fold
            acc_scr[hh] = alpha * acc_scr[hh] + _nn(p.astype(BF16), v)
            m_scr[hh] = m_new

    @pl.when(ki < qi)
    def _():
        update(False)

    @pl.when(ki == qi)
    def _():
        update(True)
        lane = lax.broadcasted_iota(jnp.int32, (tq, 2 * MLA_VDIM), 1)
        l0 = jnp.sum(l_scr[0], axis=-1, keepdims=True)
        l1 = jnp.sum(l_scr[1], axis=-1, keepdims=True)
        o = jnp.where(lane < MLA_VDIM, acc_scr[0] / l0, acc_scr[1] / l1)
        o_ref[...] = o.astype(o_ref.dtype)


def _mla_prompt(q, k, v, *, n_prompt, seq, tq=512):
    nbatch = n_prompt // seq
    nq = seq // tq
    npair = MLA_HEADS // 2
    qt, kt = _causal_tiles(nq)
    grid_spec = pltpu.PrefetchScalarGridSpec(
        num_scalar_prefetch=2,
        grid=(nbatch, npair, qt.shape[0]),
        in_specs=[pl.BlockSpec((tq, 2 * HEAD_W), lambda b, h, s, qt, kt: (b * nq + qt[s], h)),
                  pl.BlockSpec((tq, 2 * HEAD_W), lambda b, h, s, qt, kt: (b * nq + kt[s], h)),
                  pl.BlockSpec((tq, 2 * MLA_VDIM), lambda b, h, s, qt, kt: (b * nq + kt[s], h))],
        out_specs=pl.BlockSpec((tq, 2 * MLA_VDIM), lambda b, h, s, qt, kt: (b * nq + qt[s], h)),
        scratch_shapes=[pltpu.VMEM((2, tq, LANES), F32), pltpu.VMEM((2, tq, LANES), F32),
                        pltpu.VMEM((2, tq, 2 * MLA_VDIM), F32)],
    )
    return pl.pallas_call(
        _mla_flash_kernel,
        grid_spec=grid_spec,
        out_shape=jax.ShapeDtypeStruct((n_prompt, MLA_HEADS * MLA_VDIM), BF16),
        compiler_params=_params(("parallel", "parallel", "arbitrary"), 32),
        name="mla_prompt_attention",
    )(qt, kt, q, k, v)


def _absorb_kernel(q_ref, up_ref, o_ref):
    for h in range(MLA_HEADS):
        o_ref[h] = _nn(q_ref[:, HEAD_W * h:HEAD_W * (h + 1)], up_ref[h])


def _absorb(q, up, *, row0, nrows, tm=128):
    assert row0 % tm == 0 and nrows % tm == 0
    return pl.pallas_call(
        _absorb_kernel,
        grid=(nrows // tm,),
        in_specs=[pl.BlockSpec((tm, MLA_HEADS * HEAD_W), lambda i: (row0 // tm + i, 0)),
                  pl.BlockSpec(up.shape, lambda i: (0, 0, 0))],
        out_specs=pl.BlockSpec((MLA_HEADS, tm, MLA_LAT), lambda i: (0, i, 0)),
        out_shape=jax.ShapeDtypeStruct((MLA_HEADS, nrows, MLA_LAT), F32),
        compiler_params=_params(("parallel",), 32),
        name="mla_absorb_query",
    )(q, up)


def _page_copy(cache_hbm, buf, sem, page, slot, j):
    return pltpu.make_async_copy(cache_hbm.at[page], buf.at[slot, j], sem.at[slot])


class _PageStream:
    def __init__(self, pt_ref, cache_hbm, buf, sem, *, pg, nchunk):
        self.pt_ref, self.cache_hbm, self.buf, self.sem = pt_ref, cache_hbm, buf, sem
        self.pg, self.nchunk = pg, nchunk
        self.b = pl.program_id(0)
        self.nb = pl.num_programs(0)

    def slot(self, c):
        return (self.b * self.nchunk + c) % 2

    def _start(self, seq, chunk, slot):
        for j in range(self.pg):
            _page_copy(self.cache_hbm, self.buf, self.sem, self.pt_ref[seq, chunk * self.pg + j], slot, j
                       ).start(priority=j % 2)

    def prologue(self):
        @pl.when(self.b == 0)
        def _():
            self._start(0, 0, 0)

    def acquire(self, c):
        slot = self.slot(c)

        @pl.when(c + 1 < self.nchunk)
        def _():
            self._start(self.b, c + 1, 1 - slot)

        @pl.when((c + 1 == self.nchunk) & (self.b + 1 < self.nb))
        def _():
            self._start(self.b + 1, 0, 1 - slot)

        for j in range(self.pg):
            _page_copy(self.cache_hbm, self.buf, self.sem, 0, slot, j).wait()
        return slot


def _mla_sample_kernel(pt_ref, q_ref, new_ref, cache_hbm, o_ref, buf, sem, *, pg, nchunk):
    t_new = new_ref.shape[0]
    nrow = MLA_HEADS * t_new
    scale = (MLA_NOPE + MLA_ROPE) ** -0.5
    stream = _PageStream(pt_ref, cache_hbm, buf, sem, pg=pg, nchunk=nchunk)
    stream.prologue()
    q = q_ref[...].reshape(nrow, MLA_LAT).astype(BF16)

    def body(c, carry):
        m, l, acc = carry
        slot = stream.acquire(c)
        pages =[buf[slot, j].astype(BF16) for j in range(pg)]
        ss = [_nn(q, kc) * scale for kc in pages]
        smax = ss[0]
        for sj in ss[1:]:
            smax = jnp.maximum(smax, sj)
        m_new = jnp.maximum(m, jnp.max(smax, axis=-1, keepdims=True))
        alpha = jnp.exp(m - m_new)
        psum = jnp.zeros((nrow, PAGE_SIZE), F32)
        pv = jnp.zeros((nrow, MLA_KV_LORA), F32)
        for sj, kc in zip(ss, pages):
            p = jnp.exp(sj - m_new)
            psum = psum + p
            pv = pv + _nt(p.astype(BF16), kc[0:MLA_KV_LORA, :])
        l = alpha * l + jnp.sum(psum, axis=-1, keepdims=True)
        acc = alpha * acc + pv
        return m_new, l, acc

    init = (jnp.full((nrow, 1), BIG_NEG, F32), jnp.zeros((nrow, 1), F32), jnp.zeros((nrow, MLA_KV_LORA), F32))
    m, l, acc = lax.fori_loop(0, nchunk, body, init)

    kn = new_ref[...].astype(BF16)
    tq = lax.broadcasted_iota(jnp.int32, (nrow, t_new), 0) % t_new
    tk = lax.broadcasted_iota(jnp.int32, (nrow, t_new), 1)
    s = jnp.where(tk <= tq, _nt(q, kn) * scale, BIG_NEG)
    m_new = jnp.maximum(m, jnp.max(s, axis=-1, keepdims=True))
    alpha = jnp.exp(m - m_new)
    p = jnp.exp(s - m_new)
    l = alpha * l + jnp.sum(p, axis=-1, keepdims=True)
    acc = alpha * acc + _nn(p.astype(BF16), kn[:, 0:MLA_KV_LORA])
    o_ref[...] = (acc / l).reshape(MLA_HEADS, t_new, MLA_KV_LORA)


def _mla_sample(q_abs, lat, cache, page_table, *, row0, t_new, pg=16):
    nb, npages = page_table.shape
    assert npages % pg == 0 and t_new == SUBLANES and row0 % t_new == 0
    cache = jnp.transpose(cache, (0, 2, 1))
    grid_spec = pltpu.PrefetchScalarGridSpec(
        num_scalar_prefetch=1,
        grid=(nb,),
        in_specs=[pl.BlockSpec((MLA_HEADS, t_new, MLA_LAT), lambda b, pt: (0, b, 0)),
                  pl.BlockSpec((t_new, MLA_LAT), lambda b, pt: (row0 // t_new + b, 0)),
                  pl.BlockSpec(memory_space=pl.ANY)],
        out_specs=pl.BlockSpec((MLA_HEADS, t_new, MLA_KV_LORA), lambda b, pt: (0, b, 0)),
        scratch_shapes=[pltpu.VMEM((2, pg, MLA_LAT, PAGE_SIZE), F32), pltpu.SemaphoreType.DMA((2,))],
    )
    return pl.pallas_call(
        functools.partial(_mla_sample_kernel, pg=pg, nchunk=npages // pg),
        grid_spec=grid_spec,
        out_shape=jax.ShapeDtypeStruct((MLA_HEADS, nb * t_new, MLA_KV_LORA), F32),
        compiler_params=_params(("arbitrary",), 40),
        name="mla_sample_attention",
    )(page_table, q_abs, lat, cache)


def _latent_up_kernel(o_ref, w_ref, y_ref):
    outs = [_nn(o_ref[h].astype(BF16), w_ref[h]) for h in range(MLA_HEADS)]
    y_ref[...] = jnp.concatenate(outs, axis=1).astype(y_ref.dtype)


def _latent_up(o_lat, wvh, *, tm=128):
    nrows = o_lat.shape[1]
    return pl.pallas_call(
        _latent_up_kernel,
        grid=(nrows // tm,),
        in_specs=[pl.BlockSpec((MLA_HEADS, tm, MLA_KV_LORA), lambda i: (0, i, 0)),
                  pl.BlockSpec(wvh.shape, lambda i: (0, 0, 0))],
        out_specs=pl.BlockSpec((tm, MLA_HEADS * MLA_VDIM), lambda i: (i, 0)),
        out_shape=jax.ShapeDtypeStruct((nrows, MLA_HEADS * MLA_VDIM), BF16),
        compiler_params=_params(("parallel",), 32),
        name="mla_latent_up",
    )(o_lat, wvh)


def _block_mean_kernel(k_ref, o_ref, *, nblk):
    k = k_ref[...].reshape(nblk, MOBA_BLOCK, D_KV)
    o_ref[0, 0:nblk, :] = jnp.mean(k, axis=1)
    if nblk < o_ref.shape[1]:
        o_ref[0, nblk:, :] = jnp.zeros((o_ref.shape[1] - nblk, D_KV), F32)


def _block_means(u_b, *, n_prompt, seq):
    nbatch = n_prompt // seq
    nblk = seq // MOBA_BLOCK
    assert seq % MOBA_BLOCK == 0 and nblk <= LANES
    return pl.pallas_call(
        functools.partial(_block_mean_kernel, nblk=nblk),
        grid=(nbatch,),
        in_specs=[pl.BlockSpec((seq, D_KV), lambda b: (b, D_Q // D_KV))],
        out_specs=pl.BlockSpec((1, LANES, D_KV), lambda b: (b, 0, 0)),
        out_shape=jax.ShapeDtypeStruct((nbatch, LANES, D_KV), F32),
        compiler_params=_params(("parallel",), 32),
        name="moba_block_means",
    )(u_b)


def _top_blocks(gate, limit):
    lane = lax.broadcasted_iota(jnp.int32, gate.shape, 1)
    g = jnp.where(lane < limit, gate, NEG_INF)
    sel = jnp.zeros(gate.shape, F32)
    for j in range(MOBA_TOPK):
        mx = jnp.max(g, axis=-1, keepdims=True)
        idx = jnp.min(jnp.where(g == mx, lane, LANES), axis=-1, keepdims=True)
        hit = lane == idx + jnp.where(j < limit, 0, 2 * LANES)
        sel = jnp.where(hit, 1.0, sel)
        g = jnp.where(lane == idx, NEG_INF, g)
    return sel


def _moba_flash_kernel(qt_ref, kt_ref, q_ref, k_ref, v_ref, km_ref, o_ref, sel_scr, m_scr, l_scr, acc_scr):
    step = pl.program_id(2)
    qi = qt_ref[step]
    ki = kt_ref[step]
    tq, tk = q_ref.shape[0], k_ref.shape[0]
    scale = MOBA_HDIM ** -0.5
    grp = MOBA_HEADS // MOBA_KV_HEADS
    nh = 2 * grp

    @pl.when(ki == 0)
    def _():
        m_scr[...] = jnp.full(m_scr.shape, BIG_NEG, F32)
        l_scr[...] = jnp.zeros(l_scr.shape, F32)
        acc_scr[...] = jnp.zeros(acc_scr.shape, F32)
        for hh in range(nh):
            kv = hh // grp
            qh = q_ref[:, MOBA_HDIM * hh:MOBA_HDIM * (hh + 1)].astype(BF16)
            km = km_ref[0, :, MOBA_HDIM * kv:MOBA_HDIM * (kv + 1)].astype(BF16)
            sel_scr[hh] = _top_blocks(_nt(qh, km), qi)

    nhalf = tk // LANES
    masked = 2.0 * BIG_NEG

    def update(diag):
        if diag:
            row = lax.broadcasted_iota(jnp.int32, (tq, tk), 0)
            col = lax.broadcasted_iota(jnp.int32, (tq, tk), 1)
            keep_all = col <= row
        else:
            spread = jnp.where(lax.broadcasted_iota(jnp.int32, (LANES, LANES), 0) == ki, 1.0, 0.0).astype(BF16)
        for hh in range(nh):
            kv = hh // grp
            ks = slice(MOBA_HDIM * kv, MOBA_HDIM * (kv + 1))
            qh = (q_ref[:, MOBA_HDIM * hh:MOBA_HDIM * (hh + 1)] * scale).astype(BF16)
            s = _nt(qh, k_ref[:, ks].astype(BF16))
            if diag:
                keep = keep_all
            else:
                picked = _nn(sel_scr[hh].astype(BF16), spread)
                keep = jnp.concatenate([picked] * nhalf, axis=1) > 0.5
            s = jnp.where(keep, s, masked)
            s_fold = s[:, 0:LANES]
            for i in range(1, nhalf):
                s_fold = jnp.maximum(s_fold, s[:, i * LANES:(i + 1) * LANES])
            m_old = m_scr[hh]
            m_new = jnp.maximum(m_old, jnp.max(s_fold, axis=-1, keepdims=True))
            alpha = jnp.exp(m_old - m_new)
            p = jnp.exp(s - jnp.concatenate([m_new] * nhalf, axis=1))
            p_fold = p[:, 0:LANES]
            for i in range(1, nhalf):
                p_fold = p_fold + p[:, i * LANES:(i + 1) * LANES]
            l_scr[hh] = alpha * l_scr[hh] + p_fold
            acc_scr[hh] = alpha[:, 0:MOBA_HDIM] * acc_scr[hh] + _nn(p.astype(BF16), v_ref[:, ks].astype(BF16))
            m_scr[hh] = m_new

    @pl.when(ki < qi)
    def _():
        update(False)

    @pl.when(ki == qi)
    def _():
        update(True)
        outs = [acc_scr[hh] / jnp.sum(l_scr[hh], axis=-1, keepdims=True) for hh in range(nh)]
        o_ref[...] = jnp.concatenate(outs, axis=1).astype(o_ref.dtype)


def _moba_prompt(u_b, kmean, *, n_prompt, seq):
    nbatch = n_prompt // seq
    tq = MOBA_BLOCK
    nq = seq // tq
    assert math.frexp(MOBA_HDIM ** -0.5)[0] == 0.5, "score scale must be a power of two"
    npair = MOBA_KV_HEADS // 2
    grp = MOBA_HEADS // MOBA_KV_HEADS
    qw = 2 * grp * MOBA_HDIM
    kw = 2 * MOBA_HDIM
    kcb = D_Q // kw
    vcb = (D_Q + D_KV) // kw
    qt, kt = _causal_tiles(nq)
    grid_spec = pltpu.PrefetchScalarGridSpec(
        num_scalar_prefetch=2,
        grid=(nbatch, npair, qt.shape[0]),
        in_specs=[pl.BlockSpec((tq, qw), lambda b, h, s, qt, kt: (b * nq + qt[s], h)),
                  pl.BlockSpec((tq, kw), lambda b, h, s, qt, kt: (b * nq + kt[s], kcb + h)),
                  pl.BlockSpec((tq, kw), lambda b, h, s, qt, kt: (b * nq + kt[s], vcb + h)),
                  pl.BlockSpec((1, LANES, kw), lambda b, h, s, qt, kt: (b, 0, h))],
        out_specs=pl.BlockSpec((tq, qw), lambda b, h, s, qt, kt: (b * nq + qt[s], h)),
        scratch_shapes=[pltpu.VMEM((2 * grp, tq, LANES), F32), pltpu.VMEM((2 * grp, tq, LANES), F32),
                        pltpu.VMEM((2 * grp, tq, LANES), F32), pltpu.VMEM((2 * grp, tq, MOBA_HDIM), F32)],
    )
    return pl.pallas_call(
        _moba_flash_kernel,
        grid_spec=grid_spec,
        out_shape=jax.ShapeDtypeStruct((n_prompt, D_Q), BF16),
        compiler_params=_params(("parallel", "parallel", "arbitrary"), 32),
        name="moba_prompt_attention",
    )(qt, kt, u_b, u_b, u_b, kmean)


def _moba_sample_kernel(pt_ref, q_ref, kvn_ref, cache_hbm, o_ref, buf, sem, km_scr, m_scr, l_scr, o_scr,
                        *, pg, nchunk):
    t_new = q_ref.shape[0]
    grp = MOBA_HEADS // MOBA_KV_HEADS
    nrow = MOBA_HEADS * t_new
    scale = MOBA_HDIM ** -0.5
    bpc = pg * PAGE_SIZE // MOBA_BLOCK
    nblk = nchunk * bpc
    stream = _PageStream(pt_ref, cache_hbm, buf, sem, pg=pg, nchunk=nchunk)
    stream.prologue()
    q = q_ref[...]
    pieces = []
    for h in range(MOBA_HEADS):
        kv = h // grp
        parts = []
        if kv > 0:
            parts.append(jnp.zeros((t_new, MOBA_HDIM * kv), F32))
        parts.append(q[:, MOBA_HDIM * h:MOBA_HDIM * (h + 1)])
        if kv < MOBA_KV_HEADS - 1:
            parts.append(jnp.zeros((t_new, MOBA_HDIM * (MOBA_KV_HEADS - 1 - kv)), F32))
        pieces.append(jnp.concatenate(parts, axis=1))
    qbd = jnp.concatenate(pieces, axis=0).astype(BF16)
    km_scr[...] = jnp.zeros(km_scr.shape, F32)
    m_scr[...] = jnp.full(m_scr.shape, BIG_NEG, F32)
    l_scr[...] = jnp.zeros(l_scr.shape, F32)
    lane_k = lax.broadcasted_iota(jnp.int32, (D_KV, LANES), 1)
    lane_r = lax.broadcasted_iota(jnp.int32, (nrow, LANES), 1)

    def body(c, carry):
        slot = stream.acquire(c)
        ppb = MOBA_BLOCK // PAGE_SIZE
        for j in range(bpc):
            n = c * bpc + j
            ks = [buf[slot, j * ppb + i, 0:D_KV, :] for i in range(ppb)]
            ksum = ks[0]
            for kf in ks[1:]:
                ksum = ksum + kf
            kmean = jnp.sum(ksum, axis=1, keepdims=True) * (1.0 / MOBA_BLOCK)
            km_scr[...] = jnp.where(lane_k == n, kmean, km_scr[...])
            ss = [_nn(qbd, kf.astype(BF16)) * scale for kf in ks]
            smax = ss[0]
            for si in ss[1:]:
                smax = jnp.maximum(smax, si)
            m = jnp.max(smax, axis=-1, keepdims=True)
            psum = jnp.zeros((nrow, PAGE_SIZE), F32)
            pv = jnp.zeros((nrow, D_KV), F32)
            for i, si in enumerate(ss):
                p = jnp.exp(si - m)
                psum = psum + p
                pv = pv + _nt(p.astype(BF16), buf[slot, j * ppb + i, D_KV:2 * D_KV, :].astype(BF16))
            m_scr[...] = jnp.where(lane_r == n, m, m_scr[...])
            l_scr[...] = jnp.where(lane_r == n, jnp.sum(psum, axis=-1, keepdims=True), l_scr[...])
            o_scr[n] = pv
        return carry

    lax.fori_loop(0, nchunk, body, 0)

    kn = kvn_ref[:, 0:D_KV].astype(BF16)
    vn = kvn_ref[:, D_KV:2 * D_KV].astype(BF16)
    tq = lax.broadcasted_iota(jnp.int32, (nrow, t_new), 0) % t_new
    tk = lax.broadcasted_iota(jnp.int32, (nrow, t_new), 1)
    s_own = jnp.where(tk <= tq, _nt(qbd, kn) * scale, BIG_NEG)
    m_own = jnp.max(s_own, axis=-1, keepdims=True)
    p_own = jnp.exp(s_own - m_own)
    l_own = jnp.sum(p_own, axis=-1, keepdims=True)
    o_own = _nn(p_own.astype(BF16), vn)

    picked = _top_blocks(_nn(qbd, km_scr[...].astype(BF16)), nblk) > 0.5
    m_all = m_scr[...]
    mx = jnp.maximum(jnp.max(jnp.where(picked, m_all, BIG_NEG), axis=-1, keepdims=True), m_own)
    w_all = jnp.where(picked, jnp.exp(m_all - mx), 0.0)
    w_own = jnp.exp(m_own - mx)
    l = jnp.sum(w_all * l_scr[...], axis=-1, keepdims=True) + w_own * l_own
    o = w_own * o_own
    for n in range(nblk):
        o = o + w_all[:, n:n + 1] * o_scr[n]
    o = o / l
    outs = []
    for h in range(MOBA_HEADS):
        kv = h // grp
        outs.append(o[h * t_new:(h + 1) * t_new, MOBA_HDIM * kv:MOBA_HDIM * (kv + 1)])
    o_ref[...] = jnp.concatenate(outs, axis=1)


def _moba_sample(u_b, cache, page_table, *, row0, t_new, pg=8):
    nb, npages = page_table.shape
    past = npages * PAGE_SIZE
    assert npages % pg == 0 and past % MOBA_BLOCK == 0 and (pg * PAGE_SIZE) % MOBA_BLOCK == 0
    nblk = past // MOBA_BLOCK
    assert nblk <= LANES and t_new == SUBLANES
    nrow = MOBA_HEADS * t_new
    cachev = jnp.transpose(cache, (0, 2, 3, 4, 1)).reshape(cache.shape[0], 2 * D_KV, PAGE_SIZE)
    rb0 = row0 // t_new
    grid_spec = pltpu.PrefetchScalarGridSpec(
        num_scalar_prefetch=1,
        grid=(nb,),
        in_specs=[pl.BlockSpec((t_new, D_Q), lambda b, pt: (rb0 + b, 0)),
                  pl.BlockSpec((t_new, 2 * D_KV), lambda b, pt: (rb0 + b, D_Q // (2 * D_KV))),
                  pl.BlockSpec(memory_space=pl.ANY)],
        out_specs=pl.BlockSpec((t_new, D_Q), lambda b, pt: (b, 0)),
        scratch_shapes=[pltpu.VMEM((2, pg, 2 * D_KV, PAGE_SIZE), F32), pltpu.SemaphoreType.DMA((2,)),
                        pltpu.VMEM((D_KV, LANES), F32),
                        pltpu.VMEM((nrow, LANES), F32), pltpu.VMEM((nrow, LANES), F32),
                        pltpu.VMEM((nblk, nrow, D_KV), F32)],
    )
    return pl.pallas_call(
        functools.partial(_moba_sample_kernel, pg=pg, nchunk=npages // pg),
        grid_spec=grid_spec,
        out_shape=jax.ShapeDtypeStruct((nb * t_new, D_Q), F32),
        compiler_params=_params(("arbitrary",), 40),
        name="moba_sample_attention",
    )(page_table, u_b, u_b, cachev)


def _odd_out_kernel(ocp_ref, ocs_ref, odp_ref, ods_ref, x_ref, w_ref, g_ref, b_ref, y_ref, *, npb):
    half = MLA_HEADS * MLA_VDIM

    def compute(oc_ref, od_ref):
        m = (_nn(oc_ref[...].astype(BF16), w_ref[0:half, :])
             + _nn(od_ref[...].astype(BF16), w_ref[half:half + D_Q, :]))
        y_ref[...] = _layer_norm_rows(DN_ALPHA * x_ref[...] + m, g_ref[...], b_ref[...])

    i = pl.program_id(0)

    @pl.when(i < npb)
    def _():
        compute(ocp_ref, odp_ref)

    @pl.when(i >= npb)
    def _():
        compute(ocs_ref, ods_ref)


def _odd_out(oc, od, x, w_bf16, g, b, *, n_prompt, tm=512):
    n = x.shape[0]
    assert n % tm == 0 and n_prompt % tm == 0
    npb = n_prompt // tm
    const = lambda a: pl.BlockSpec(a.shape, lambda i: (0, 0))
    in_specs = (_two_source_specs(oc[0].shape[1], tm, npb) + _two_source_specs(od[0].shape[1], tm, npb)
                + [pl.BlockSpec((tm, D_MODEL), lambda i: (i, 0)), const(w_bf16), const(g), const(b)])
    return pl.pallas_call(
        functools.partial(_odd_out_kernel, npb=npb),
        grid=(n // tm,),
        in_specs=in_specs,
        out_specs=pl.BlockSpec((tm, D_MODEL), lambda i: (i, 0)),
        out_shape=jax.ShapeDtypeStruct((n, D_MODEL), F32),
        compiler_params=_params(("arbitrary",), 40),
        name="odd_out_deepnorm",
    )(*oc, *od, x, w_bf16, g, b)


def _odd_layer(x, cache_mla, cache_moba, page_table, w_in, gq, w_uq, gkv, w_uk, w_uv, w_out, ln_g, ln_b,
               *, n_prompt, seq, t_new):
    n_tok = x.shape[0]
    n_sample = n_tok - n_prompt
    past = page_table.shape[1] * PAGE_SIZE
    w_in2, wqa, wqb, wk, wv, up, wvh = _odd_weights(w_in, w_uq, w_uk, w_uv)
    wa = MLA_KV_LORA + HEAD_W + MLA_Q_LORA
    u_a, u_b = _matmul(x, w_in2, splits=(wa, D_Q + 2 * D_KV))
    pos = jnp.concatenate([jnp.tile(jnp.arange(seq, dtype=jnp.int32), n_prompt // seq),
                           jnp.tile(past + jnp.arange(t_new, dtype=jnp.int32), n_sample // t_new)])
    cos_t, sin_t = _rope_tables(pos)
    lat, q, k, v = _odd_prep(u_a, cos_t, sin_t, gq.reshape(1, -1), gkv.reshape(1, -1), wqa, wqb, wk, wv)
    oc_p = _mla_prompt(q, k, v, n_prompt=n_prompt, seq=seq)
    q_abs = _absorb(q, up, row0=n_prompt, nrows=n_sample)
    o_lat = _mla_sample(q_abs, lat, cache_mla, page_table, row0=n_prompt, t_new=t_new)
    oc_s = _latent_up(o_lat, wvh)
    kmean = _block_means(u_b, n_prompt=n_prompt, seq=seq)
    od_p = _moba_prompt(u_b, kmean, n_prompt=n_prompt, seq=seq)
    od_s = _moba_sample(u_b, cache_moba, page_table, row0=n_prompt, t_new=t_new)
    w_out_b = w_out.astype(BF16)
    g2, b2 = ln_g.reshape(1, -1), ln_b.reshape(1, -1)
    y = _odd_out((oc_p, oc_s), (od_p, od_s), x, w_out_b, g2, b2, n_prompt=n_prompt)
    kv_new = u_b[:, D_Q:]
    return y, lat, kv_new


def kernel(x_prompt, x_sample, state_hgrn, cache_swa_w128, cache_swa_w512, cache_swa_w2048, cache_mla, cache_moba_kv, page_table, w_in_even, hgrn_lb_logits, hgrn_norm_g, w_out_even, w_in_odd, mla_q_norm_g, mla_w_uq, mla_kv_norm_g, mla_w_uk, mla_w_uv, w_out_odd, ln_mix_g, ln_mix_b, ln_ffn_g, ln_ffn_b, router_group_w, router_group_b, router_expert_w, router_expert_b, expert_w_gate, expert_w_up, expert_w_down):
    nbatch, seq, _ = x_prompt.shape
    nb_s, t_new, _ = x_sample.shape
    n_prompt = nbatch * seq
    x = jnp.concatenate([x_prompt.reshape(n_prompt, D_MODEL), x_sample.reshape(nb_s * t_new, D_MODEL)], axis=0)
    caches = [cache_swa_w128, cache_swa_w512, cache_swa_w2048]
    x, hgrn_p, hgrn_s, swa_p, swa_s = _even_layer(
        x, state_hgrn, caches, w_in_even, hgrn_lb_logits, hgrn_norm_g, w_out_even, ln_mix_g[0], ln_mix_b[0],
        n_prompt=n_prompt, seq=seq, t_new=t_new)

    def moe(x, layer):
        wr, br = _router_weights(router_group_w[layer], router_group_b[layer],
                                 router_expert_w[layer], router_expert_b[layer])
        return _hier_moe(x, wr, br, expert_w_gate[layer].astype(BF16), expert_w_up[layer].astype(BF16),
                         expert_w_down[layer].astype(BF16),
                         ln_ffn_g[layer].reshape(1, -1), ln_ffn_b[layer].reshape(1, -1))

    x = moe(x, 0)
    x, lat, kv_new = _odd_layer(
        x, cache_mla, cache_moba_kv, page_table, w_in_odd, mla_q_norm_g, mla_w_uq, mla_kv_norm_g, mla_w_uk,
        mla_w_uv, w_out_odd, ln_mix_g[1], ln_mix_b[1], n_prompt=n_prompt, seq=seq, t_new=t_new)
    x = moe(x, 1)
    kv_shape = (2, MOBA_KV_HEADS, MOBA_HDIM)
    return (x[:n_prompt].reshape(x_prompt.shape), x[n_prompt:].reshape(x_sample.shape),
            hgrn_p, hgrn_s, swa_p[0], swa_s[0], swa_p[1], swa_s[1], swa_p[2], swa_s[2],
            lat[:n_prompt].reshape(nbatch, seq, MLA_LAT), lat[n_prompt:].reshape(nb_s, t_new, MLA_LAT),
            kv_new[:n_prompt].reshape((nbatch, seq) + kv_shape), kv_new[n_prompt:].reshape((nb_s, t_new) + kv_shape))
```

```python
import functools
import math

import jax
import jax.numpy as jnp
from jax import lax
from jax.experimental import pallas as pl
from jax.experimental.pallas import tpu as pltpu

F32 = jnp.float32
BF16 = jnp.bfloat16

D_MODEL = 1024
DEPTH = 2
PAGE_SIZE = 128
HGRN_HEADS = 4
HGRN_KDIM = 128
HGRN_VDIM = 128
HGRN_CHUNK = 32
DIL_GROUPS = ((128, 1), (512, 4), (2048, 16))
DIL_HEADS = 4
DIL_HDIM = 64
MLA_HEADS = 8
MLA_Q_LORA = 384
MLA_KV_LORA = 256
MLA_NOPE = 64
MLA_ROPE = 32
MLA_VDIM = 64
ROPE_THETA = 10000.0
MOBA_HEADS = 8
MOBA_KV_HEADS = 4
MOBA_HDIM = 64
MOBA_BLOCK = 256
MOBA_TOPK = 3
MOE_GROUPS = 4
MOE_EXPERTS_PER_GROUP = 8
MOE_EXPERTS = MOE_GROUPS * MOE_EXPERTS_PER_GROUP
MOE_TOPK = 2
MOE_FF = 512
DN_ALPHA = (2 * DEPTH) ** 0.25
LN_EPS = 1e-5
RMS_EPS = 1e-6

A_QF = HGRN_HEADS * HGRN_KDIM
A_IV = HGRN_HEADS * HGRN_VDIM
B_W = DIL_HEADS * DIL_HDIM
EVEN_IN = 2 * A_QF + 2 * A_IV + 3 * len(DIL_GROUPS) * B_W
MLA_LAT = MLA_KV_LORA + MLA_ROPE
D_Q = MOBA_HEADS * MOBA_HDIM
D_KV = MOBA_KV_HEADS * MOBA_HDIM

LANES = 128
SUBLANES = 8
MIB = 2 ** 20

NEG_INF = float("-inf")


def _params(semantics, vmem_mib):
    return pltpu.CompilerParams(dimension_semantics=semantics, vmem_limit_bytes=vmem_mib * MIB)


def _nt(a, b):
    return lax.dot_general(a, b, (((1,), (1,)), ((), ())), preferred_element_type=F32)


def _tn(a, b):
    return lax.dot_general(a, b, (((0,), (0,)), ((), ())), preferred_element_type=F32)


def _nn(a, b):
    return jnp.dot(a, b, preferred_element_type=F32)


def _layer_norm_rows(z, g, b):
    zc = z - jnp.mean(z, axis=-1, keepdims=True)
    y = zc * lax.rsqrt(jnp.mean(zc * zc, axis=-1, keepdims=True) + LN_EPS)
    return y * g + b


def _matmul_kernel(x_ref, w_ref, *refs, tn, nsplit, residue_views):
    o_refs = refs[:nsplit]
    xb = x_ref[...].astype(BF16)
    base = 0
    for o_ref in o_refs:
        n = o_ref.shape[1]
        for c0 in range(0, n, tn):
            c1 = min(c0 + tn, n)
            o_ref[:, c0:c1] = _nn(xb, w_ref[:, base + c0:base + c1]).astype(o_ref.dtype)
        base += n
    if not residue_views:
        return
    stage = refs[-1]
    tm = x_ref.shape[0]
    for (c0, width, dil), r_ref in zip(residue_views, refs[nsplit:-1]):
        for cb in range(width // LANES):
            stage[...] = o_refs[0][:, c0 + cb * LANES:c0 + (cb + 1) * LANES]
            for r in range(dil):
                r_ref[:, r * width + cb * LANES:r * width + (cb + 1) * LANES] = (
                    stage[pl.ds(r, tm // dil, stride=dil), :])


def _matmul(x, w_bf16, *, splits=None, residue_views=(), tm=256, tn=512):
    m, k = x.shape
    n = w_bf16.shape[1]
    splits = splits or (n,)
    assert m % tm == 0 and sum(splits) == n and all(s % LANES == 0 for s in splits)
    assert all(tm % (SUBLANES * d) == 0 and wd % LANES == 0 for _, wd, d in residue_views)
    out_specs = [pl.BlockSpec((tm, s), lambda i: (i, 0)) for s in splits]
    out_shape = [jax.ShapeDtypeStruct((m, s), F32) for s in splits]
    for _, wd, d in residue_views:
        out_specs.append(pl.BlockSpec((tm // d, d * wd), lambda i: (i, 0)))
        out_shape.append(jax.ShapeDtypeStruct((m // d, d * wd), F32))
    outs = pl.pallas_call(
        functools.partial(_matmul_kernel, tn=tn, nsplit=len(splits), residue_views=tuple(residue_views)),
        grid=(m // tm,),
        in_specs=[pl.BlockSpec((tm, k), lambda i: (i, 0)),
                  pl.BlockSpec((k, n), lambda i: (0, 0))],
        out_specs=out_specs,
        out_shape=out_shape,
        scratch_shapes=[pltpu.VMEM((tm, LANES), F32)] if residue_views else [],
        compiler_params=_params(("parallel",), 48),
        name="dense_projection",
    )(x, w_bf16)
    return outs[0] if len(outs) == 1 else outs


def _chunk_cumsum(x, c):
    row = lax.broadcasted_iota(jnp.int32, x.shape, 0) & (c - 1)
    y = x
    s = 1
    while s < c:
        y = y + jnp.where(row >= s, pltpu.roll(y, s, 0), 0.0)
        s *= 2
    return y


def _hgrn_kernel(q_ref, f_ref, i_ref, g_ref, lb_ref, ng_ref, s0_ref, o_ref, sfin_ref, st_scr,
                 *, c, sb, nsb):
    t = pl.program_id(1)

    @pl.when(t == 0)
    def _():
        for h in range(HGRN_HEADS):
            st_scr[h] = s0_ref[0, h].T

    nchunk = sb // c
    row = lax.broadcasted_iota(jnp.int32, (sb, sb), 0)
    col = lax.broadcasted_iota(jnp.int32, (sb, sb), 1)
    causal = (col <= row) & ((row // c) == (col // c))
    ng = ng_ref[...]

    def sub(j, carry):
        rows = pl.ds(pl.multiple_of(j * sb, sb), sb)
        for h in range(HGRN_HEADS):
            cs = slice(HGRN_KDIM * h, HGRN_KDIM * (h + 1))
            q = q_ref[rows, cs]
            fp = f_ref[rows, cs]
            v = i_ref[rows, cs]
            g = g_ref[rows, cs]
            lb = lb_ref[:, cs]
            lf = jnp.log(lb + (1.0 - lb) * jax.nn.sigmoid(fp))
            k = (1.0 - lb) * jax.nn.sigmoid(-fp)
            cum = _chunk_cumsum(lf, c)
            last = jnp.concatenate(
                [jnp.broadcast_to(cum[ci * c + c - 1:ci * c + c, :], (c, HGRN_KDIM)) for ci in range(nchunk)],
                axis=0)
            qd = q * jnp.exp(cum)
            kd = k * jnp.exp(-cum)
            kl = k * jnp.exp(last - cum)
            att = jnp.where(causal, _nt(qd, kd), 0.0)
            intra = _nn(att, v)
            st = st_scr[h]
            outs = []
            for ci in range(nchunk):
                r = slice(ci * c, (ci + 1) * c)
                outs.append(intra[r] + _nt(qd[r], st))
                st = st * jnp.exp(last[ci * c:ci * c + 1, :]) + _tn(v[r], kl[r])
            st_scr[h] = st
            o = jnp.concatenate(outs, axis=0) if nchunk > 1 else outs[0]
            y = o * lax.rsqrt(jnp.mean(o * o, axis=-1, keepdims=True) + RMS_EPS) * ng
            o_ref[rows, cs] = (y * (g * jax.nn.sigmoid(g))).astype(o_ref.dtype)
        return carry

    lax.fori_loop(0, nsb, sub, 0)

    @pl.when(t == pl.num_programs(1) - 1)
    def _():
        for h in range(HGRN_HEADS):
            sfin_ref[0, h] = st_scr[h].T


def _hgrn(u, lb, norm_g, s0, *, row0, seq, tb, c):
    nb = s0.shape[0]
    nt = seq // tb
    sb = min(tb, LANES)
    assert seq % tb == 0 and tb % sb == 0 and sb % c == 0 and row0 % tb == 0
    rb0 = row0 // tb
    in_specs = [pl.BlockSpec((tb, A_QF), lambda b, t, j=j: (rb0 + b * nt + t, j)) for j in range(4)]
    in_specs += [pl.BlockSpec((1, A_QF), lambda b, t: (0, 0)),
                 pl.BlockSpec((1, HGRN_VDIM), lambda b, t: (0, 0)),
                 pl.BlockSpec((1, HGRN_HEADS, HGRN_KDIM, HGRN_VDIM), lambda b, t: (b, 0, 0, 0))]
    return pl.pallas_call(
        functools.partial(_hgrn_kernel, c=c, sb=sb, nsb=tb // sb),
        grid=(nb, nt),
        in_specs=in_specs,
        out_specs=[pl.BlockSpec((tb, A_IV), lambda b, t: (b * nt + t, 0)),
                   pl.BlockSpec((1, HGRN_HEADS, HGRN_KDIM, HGRN_VDIM), lambda b, t: (b, 0, 0, 0))],
        out_shape=[jax.ShapeDtypeStruct((nb * seq, A_IV), F32),
                   jax.ShapeDtypeStruct(s0.shape, F32)],
        scratch_shapes=[pltpu.VMEM((HGRN_HEADS, HGRN_VDIM, HGRN_KDIM), F32)],
        compiler_params=_params(("parallel", "arbitrary"), 48),
        name="hgrn2_scan",
    )(u, u, u, u, lb, norm_g, s0)


def _dil_prompt_kernel(q_ref, kc_ref, vc_ref, kp_ref, vp_ref, o_ref, l_ref):
    w = q_ref.shape[0]
    n = pl.program_id(1)
    a = lax.broadcasted_iota(jnp.int32, (w, 2 * w), 0)
    cc = lax.broadcasted_iota(jnp.int32, (w, 2 * w), 1)
    lo = a + jnp.where(n > 0, 0, 2 * w)
    valid = ((cc >= w) & ((cc - w) <= a)) | ((cc < w) & (cc >= lo))
    scale = DIL_HDIM ** -0.5
    outs, lses = [], []
    for h in range(DIL_HEADS):
        cs = slice(DIL_HDIM * h, DIL_HDIM * (h + 1))
        q = q_ref[:, cs].astype(BF16)
        k = jnp.concatenate([kp_ref[:, cs], kc_ref[:, cs]], axis=0).astype(BF16)
        v = jnp.concatenate([vp_ref[:, cs], vc_ref[:, cs]], axis=0).astype(BF16)
        s = jnp.where(valid, _nt(q, k) * scale, NEG_INF)
        m = jnp.max(s, axis=-1, keepdims=True)
        e = jnp.exp(s - m)
        l = jnp.sum(e, axis=-1, keepdims=True)
        outs.append(_nn(e.astype(BF16), v) / l)
        lses.append(jnp.broadcast_to(m + jnp.log(l), (w, DIL_HDIM)))
    o_ref[...] = jnp.concatenate(outs, axis=1)
    l_ref[...] = jnp.concatenate(lses, axis=1)


def _dil_prompt(uv, grp, *, n_prompt, seq, ncb, cb0):
    win, dil = DIL_GROUPS[grp]
    w = win // dil
    unit = dil * w
    assert seq % unit == 0 and w == LANES and uv.shape[1] == dil * ncb * B_W
    nbatch = n_prompt // seq
    nblk = seq // unit

    def spec(cb, prev):
        def imap(b, n, r):
            nn_ = jnp.maximum(n - 1, 0) if prev else n
            return (b * nblk + nn_, r * ncb + cb)
        return pl.BlockSpec((w, B_W), imap)

    out_spec = pl.BlockSpec((w, B_W), lambda b, n, r: (b * nblk + n, r))
    o, lse = pl.pallas_call(
        _dil_prompt_kernel,
        grid=(nbatch, nblk, dil),
        in_specs=[spec(cb0, False), spec(cb0 + 1, False), spec(cb0 + 2, False),
                  spec(cb0 + 1, True), spec(cb0 + 2, True)],
        out_specs=[out_spec, out_spec],
        out_shape=[jax.ShapeDtypeStruct((n_prompt // dil, dil * B_W), F32)] * 2,
        compiler_params=_params(("parallel", "arbitrary", "arbitrary"), 32),
        name=f"dilated_prompt_w{win}",
    )(uv, uv, uv, uv, uv)
    return o.reshape(n_prompt, B_W), lse.reshape(n_prompt, B_W)


def _dil_sample_kernel(q_ref, k_ref, v_ref, buf_ref, nbuf_ref, o_ref, l_ref, *, dil):
    lb = buf_ref.shape[2]
    t_new = q_ref.shape[0]
    nrow = DIL_HEADS * t_new
    q = q_ref[...]
    lane_head = lax.broadcasted_iota(jnp.int32, (t_new, B_W), 1) // DIL_HDIM
    qbd = jnp.concatenate([jnp.where(lane_head == h, q, 0.0) for h in range(DIL_HEADS)], axis=0).astype(BF16)
    scale = DIL_HDIM ** -0.5

    old = buf_ref[0]
    k_old = old[0:B_W].astype(BF16)
    v_old = old[B_W:2 * B_W].astype(BF16)
    k_new = k_ref[...].astype(BF16)
    v_new = v_ref[...].astype(BF16)

    tq = lax.broadcasted_iota(jnp.int32, (nrow, lb), 0) % t_new
    d_old = lax.broadcasted_iota(jnp.int32, (nrow, lb), 1) - tq
    valid_m = (d_old >= 0) & ((d_old & (dil - 1)) == 0)
    th = lax.broadcasted_iota(jnp.int32, (nrow, t_new), 0) % t_new
    d_new = th - lax.broadcasted_iota(jnp.int32, (nrow, t_new), 1)
    valid_h = (d_new >= 0) & ((d_new & (dil - 1)) == 0)

    s_m = jnp.where(valid_m, _nn(qbd, k_old) * scale, NEG_INF)
    s_h = jnp.where(valid_h, _nt(qbd, k_new) * scale, NEG_INF)
    m = jnp.maximum(jnp.max(s_m, axis=-1, keepdims=True), jnp.max(s_h, axis=-1, keepdims=True))
    e_m = jnp.exp(s_m - m)
    e_h = jnp.exp(s_h - m)
    l = jnp.sum(e_m, axis=-1, keepdims=True) + jnp.sum(e_h, axis=-1, keepdims=True)
    o_all = (_nt(e_m.astype(BF16), v_old) + _nn(e_h.astype(BF16), v_new)) / l

    nbuf_ref[0] = pltpu.roll(old, lb - t_new, 1)
    kv = jnp.concatenate([k_ref[...], v_ref[...]], axis=1)
    kv_t = jnp.concatenate([kv, jnp.zeros((LANES - t_new, 2 * B_W), F32)], axis=0).T
    lane = lax.broadcasted_iota(jnp.int32, (2 * B_W, LANES), 1)
    tail = jnp.where(lane >= LANES - t_new, pltpu.roll(kv_t, LANES - t_new, 1), nbuf_ref[0, :, lb - LANES:lb])
    nbuf_ref[0, :, lb - LANES:lb] = tail
    lse_all = jnp.broadcast_to(m + jnp.log(l), (nrow, B_W))
    o = jnp.zeros((t_new, B_W), F32)
    lse = jnp.zeros((t_new, B_W), F32)
    for h in range(DIL_HEADS):
        r = slice(h * t_new, (h + 1) * t_new)
        o = jnp.where(lane_head == h, o_all[r], o)
        lse = jnp.where(lane_head == h, lse_all[r], lse)
    o_ref[...] = o
    l_ref[...] = lse


def _dil_sample(u, buf, grp, *, n_prompt, t_new):
    win, dil = DIL_GROUPS[grp]
    nb, lb = buf.shape[0], buf.shape[1]
    assert lb == win and t_new == SUBLANES and n_prompt % t_new == 0
    cb0 = (2 * A_QF + 2 * A_IV) // B_W + 3 * grp
    rb0 = n_prompt // t_new
    bufv = jnp.transpose(buf, (0, 2, 3, 4, 1)).reshape(nb, 2 * B_W, lb)

    def uspec(cb):
        return pl.BlockSpec((t_new, B_W), lambda b: (rb0 + b, cb))

    tok_spec = pl.BlockSpec((t_new, B_W), lambda b: (b, 0))
    nbuf, o, lse = pl.pallas_call(
        functools.partial(_dil_sample_kernel, dil=dil),
        grid=(nb,),
        in_specs=[uspec(cb0), uspec(cb0 + 1), uspec(cb0 + 2),
                  pl.BlockSpec((1, 2 * B_W, lb), lambda b: (b, 0, 0))],
        out_specs=[pl.BlockSpec((1, 2 * B_W, lb), lambda b: (b, 0, 0)), tok_spec, tok_spec],
        out_shape=[jax.ShapeDtypeStruct(bufv.shape, F32),
                   jax.ShapeDtypeStruct((nb * t_new, B_W), F32),
                   jax.ShapeDtypeStruct((nb * t_new, B_W), F32)],
        compiler_params=_params(("parallel",), 48),
        name=f"dilated_sample_w{win}",
    )(u, u, u, bufv)
    nbuf = jnp.transpose(nbuf.reshape(nb, 2, DIL_HEADS, DIL_HDIM, lb), (0, 4, 1, 2, 3))
    return nbuf, o, lse


def _two_source_specs(width, tm, npb):
    return [pl.BlockSpec((tm, width), lambda i: (jnp.minimum(i, npb - 1), 0)),
            pl.BlockSpec((tm, width), lambda i: (jnp.maximum(i - npb, 0), 0))]


def _even_out_kernel(*refs, npb):
    srcs = refs[:14]
    x_ref, w_ref, g_ref, b_ref, y_ref = refs[14:]

    def compute(ga_ref, os_, ls_):
        lses = [l[...] for l in ls_]
        mx = jnp.maximum(jnp.maximum(lses[0], lses[1]), lses[2])
        es = [jnp.exp(l - mx) for l in lses]
        tot = es[0] + es[1] + es[2]
        ob = (es[0] / tot) * os_[0][...] + (es[1] / tot) * os_[1][...] + (es[2] / tot) * os_[2][...]
        m = _nn(ga_ref[...].astype(BF16), w_ref[0:A_IV, :]) + _nn(ob.astype(BF16), w_ref[A_IV:A_IV + B_W, :])
        y_ref[...] = _layer_norm_rows(DN_ALPHA * x_ref[...] + m, g_ref[...], b_ref[...])

    i = pl.program_id(0)
    for src, cond in ((0, i < npb), (1, i >= npb)):
        @pl.when(cond)
        def _(src=src):
            compute(srcs[src], srcs[2 + src:8:2], srcs[8 + src:14:2])


def _even_out(gated, os_, lses, x, w_bf16, g, b, *, n_prompt, tm=512):
    n = x.shape[0]
    assert n % tm == 0 and n_prompt % tm == 0
    npb = n_prompt // tm
    const = lambda shape: pl.BlockSpec(shape, lambda i: (0, 0))
    in_specs = _two_source_specs(A_IV, tm, npb)
    args = list(gated)
    for pair in list(os_) + list(lses):
        in_specs += _two_source_specs(B_W, tm, npb)
        args += list(pair)
    in_specs += [pl.BlockSpec((tm, D_MODEL), lambda i: (i, 0)), const(w_bf16.shape),
                 const((1, D_MODEL)), const((1, D_MODEL))]
    return pl.pallas_call(
        functools.partial(_even_out_kernel, npb=npb),
        grid=(n // tm,),
        in_specs=in_specs,
        out_specs=pl.BlockSpec((tm, D_MODEL), lambda i: (i, 0)),
        out_shape=jax.ShapeDtypeStruct((n, D_MODEL), F32),
        compiler_params=_params(("arbitrary",), 40),
        name="even_out_deepnorm",
    )(*args, x, w_bf16, g, b)


def _router_kernel(x_ref, w_ref, b_ref, e_ref, p_ref):
    logits = _nn(x_ref[...].astype(BF16), w_ref[...]) + b_ref[...]
    tm = logits.shape[0]
    lane = lax.broadcasted_iota(jnp.int32, logits.shape, 1)
    big = jnp.int32(LANES)
    is_g = lane < MOE_GROUPS
    gl = jnp.where(is_g, logits, NEG_INF)
    gmax = jnp.max(gl, axis=-1, keepdims=True)
    gsel = jnp.min(jnp.where(is_g & (gl == gmax), lane, big), axis=-1, keepdims=True)
    gw = 1.0 / jnp.sum(jnp.where(is_g, jnp.exp(gl - gmax), 0.0), axis=-1, keepdims=True)
    e_id = lane - MOE_GROUPS
    in_grp = (e_id >= gsel * MOE_EXPERTS_PER_GROUP) & (e_id < (gsel + 1) * MOE_EXPERTS_PER_GROUP)
    el = jnp.where(in_grp, logits, NEG_INF)
    v1 = jnp.max(el, axis=-1, keepdims=True)
    i1 = jnp.min(jnp.where(in_grp & (el == v1), e_id, big), axis=-1, keepdims=True)
    el2 = jnp.where(e_id == i1, NEG_INF, el)
    v2 = jnp.max(el2, axis=-1, keepdims=True)
    i2 = jnp.min(jnp.where(in_grp & (e_id != i1) & (el2 == v2), e_id, big), axis=-1, keepdims=True)
    e2 = jnp.exp(v2 - v1)
    den = 1.0 + e2
    w1 = gw * (1.0 / den)
    w2 = gw * (e2 / den)
    e_ref[...] = jnp.where(lane == 0, i1, jnp.where(lane == 1, i2, 0))
    p_ref[...] = jnp.where(lane == 0, w1, jnp.where(lane == 1, w2, 0.0))
    del tm


def _router(x, w_bf16, bias, *, tm=512):
    n = x.shape[0]
    row = lambda dt: pl.BlockSpec((tm, LANES), lambda i: (i, 0))
    return pl.pallas_call(
        _router_kernel,
        grid=(n // tm,),
        in_specs=[pl.BlockSpec((tm, D_MODEL), lambda i: (i, 0)),
                  pl.BlockSpec((D_MODEL, LANES), lambda i: (0, 0)),
                  pl.BlockSpec((1, LANES), lambda i: (0, 0))],
        out_specs=[row(jnp.int32), row(F32)],
        out_shape=[jax.ShapeDtypeStruct((n, LANES), jnp.int32), jax.ShapeDtypeStruct((n, LANES), F32)],
        compiler_params=_params(("parallel",), 32),
        name="moe_router",
    )(x, w_bf16, bias)


def _expert_kernel(blk_e_ref, src_ref, dst_ref, x_hbm, wg_ref, wu_ref, wd_ref, y_hbm,
                   xg, yo, gsem, ssem, *, rows):
    del blk_e_ref
    i = pl.program_id(0)
    nsteps = pl.num_programs(0)
    slot = i % 2

    def gather_start(step, sl):
        base = step * rows
        for r in range(rows):
            pltpu.make_async_copy(x_hbm.at[pl.ds(src_ref[base + r], 1)], xg.at[sl, pl.ds(r, 1)],
                                  gsem.at[sl]).start()

    def scatter_start(step, sl):
        base = step * rows
        for r in range(rows):
            pltpu.make_async_copy(yo.at[sl, pl.ds(r, 1)], y_hbm.at[pl.ds(dst_ref[base + r], 1)],
                                  ssem.at[sl]).start()

    def gather_wait(sl):
        pltpu.make_async_copy(x_hbm.at[pl.ds(0, rows)], xg.at[sl], gsem.at[sl]).wait()

    def scatter_wait(sl):
        pltpu.make_async_copy(yo.at[sl], y_hbm.at[pl.ds(0, rows)], ssem.at[sl]).wait()

    @pl.when(i == 0)
    def _():
        gather_start(0, 0)
        spare = y_hbm.shape[0] - 2 * rows
        for sl in range(2):
            yo[sl] = jnp.zeros((rows, D_MODEL), F32)
            fill = pltpu.make_async_copy(yo.at[sl], y_hbm.at[pl.ds(spare + sl * rows, rows)], ssem.at[sl])
            fill.start()
            fill.wait()

    @pl.when(i + 1 < nsteps)
    def _():
        gather_start(i + 1, 1 - slot)

    gather_wait(slot)

    @pl.when(i >= 2)
    def _():
        scatter_wait(slot)

    xb = xg[slot].astype(BF16)
    hdn = _nn(xb, wg_ref[0])
    hdn = (hdn * jax.nn.sigmoid(hdn)) * _nn(xb, wu_ref[0])
    yo[slot] = _nn(hdn.astype(BF16), wd_ref[0])
    scatter_start(i, slot)

    @pl.when(i == nsteps - 1)
    def _():
        scatter_wait(slot)

        @pl.when(nsteps >= 2)
        def _():
            scatter_wait(1 - slot)


def _experts(x, blk_e, src, dst, wg, wu, wd, *, rows):
    n = x.shape[0]
    nblk = blk_e.shape[0]
    wspec = lambda shape: pl.BlockSpec((1,) + shape, lambda i, be, s, d: (be[i], 0, 0))
    grid_spec = pltpu.PrefetchScalarGridSpec(
        num_scalar_prefetch=3,
        grid=(nblk,),
        in_specs=[pl.BlockSpec(memory_space=pl.ANY),
                  wspec((D_MODEL, MOE_FF)), wspec((D_MODEL, MOE_FF)), wspec((MOE_FF, D_MODEL))],
        out_specs=pl.BlockSpec(memory_space=pl.ANY),
        scratch_shapes=[pltpu.VMEM((2, rows, D_MODEL), F32), pltpu.VMEM((2, rows, D_MODEL), F32),
                        pltpu.SemaphoreType.DMA((2,)), pltpu.SemaphoreType.DMA((2,))],
    )
    return pl.pallas_call(
        functools.partial(_expert_kernel, rows=rows),
        grid_spec=grid_spec,
        out_shape=jax.ShapeDtypeStruct((MOE_TOPK * n + 2 * rows, D_MODEL), F32),
        compiler_params=_params(("arbitrary",), 40),
        name="moe_experts",
    )(blk_e, src, dst, x, wg, wu, wd)


def _moe_combine_kernel(y0_ref, y1_ref, p_ref, x_ref, g_ref, b_ref, o_ref):
    p = p_ref[...]
    f = y0_ref[...] * p[:, 0:1] + y1_ref[...] * p[:, 1:2]
    o_ref[...] = _layer_norm_rows(DN_ALPHA * x_ref[...] + f, g_ref[...], b_ref[...])


def _moe_combine(y2, p, x, g, b, *, tm=512):
    n = x.shape[0]
    assert n % tm == 0
    return pl.pallas_call(
        _moe_combine_kernel,
        grid=(n // tm,),
        in_specs=[pl.BlockSpec((tm, D_MODEL), lambda i: (i, 0)),
                  pl.BlockSpec((tm, D_MODEL), lambda i: (n // tm + i, 0)),
                  pl.BlockSpec((tm, LANES), lambda i: (i, 0)),
                  pl.BlockSpec((tm, D_MODEL), lambda i: (i, 0)),
                  pl.BlockSpec((1, D_MODEL), lambda i: (0, 0)),
                  pl.BlockSpec((1, D_MODEL), lambda i: (0, 0))],
        out_specs=pl.BlockSpec((tm, D_MODEL), lambda i: (i, 0)),
        out_shape=jax.ShapeDtypeStruct((n, D_MODEL), F32),
        compiler_params=_params(("parallel",), 40),
        name="moe_combine_deepnorm",
    )(y2, y2, p, x, g, b)


def _dispatch_plan(eidx, *, rows):
    n = eidx.shape[0]
    na = n * MOE_TOPK
    flat_e = eidx.T.reshape(-1)
    order = jnp.argsort(flat_e).astype(jnp.int32)
    se = flat_e[order]
    counts = jnp.bincount(flat_e, length=MOE_EXPERTS).astype(jnp.int32)
    padded = (counts + rows - 1) // rows * rows
    pad_end = jnp.cumsum(padded)
    pad_start = pad_end - padded
    start = jnp.cumsum(counts) - counts
    dest = pad_start[se] + jnp.arange(na, dtype=jnp.int32) - start[se]
    nblk = -(-(na + MOE_EXPERTS * (rows - 1)) // rows)
    dst = jnp.full((nblk * rows,), -1, jnp.int32).at[dest].set(order)
    pad = dst < 0
    src = jnp.where(pad, 0, dst % n)
    dst = jnp.where(pad, na + jnp.arange(nblk * rows, dtype=jnp.int32) % (2 * rows), dst)
    blk_e = jnp.minimum(jnp.searchsorted(pad_end, jnp.arange(nblk, dtype=jnp.int32) * rows, side="right"),
                        MOE_EXPERTS - 1).astype(jnp.int32)
    return blk_e, src, dst


def _hier_moe(x, wr_bf16, br, wg, wu, wd, ln_g, ln_b, *, rows=256):
    e_pad, p_pad = _router(x, wr_bf16, br)
    blk_e, src, dst = _dispatch_plan(e_pad[:, :MOE_TOPK], rows=rows)
    y2 = _experts(x, blk_e, src, dst, wg, wu, wd, rows=rows)
    return _moe_combine(y2, p_pad, x, ln_g, ln_b)


def _router_weights(wg, bg, we, be):
    w = jnp.zeros((D_MODEL, LANES), F32).at[:, :MOE_GROUPS].set(wg).at[:, MOE_GROUPS:MOE_GROUPS + MOE_EXPERTS].set(we)
    b = jnp.zeros((1, LANES), F32).at[0, :MOE_GROUPS].set(bg).at[0, MOE_GROUPS:MOE_GROUPS + MOE_EXPERTS].set(be)
    return w.astype(BF16), b


def _even_layer(x, state_hgrn, caches, w_in, lb_logits, norm_g, w_out, ln_g, ln_b, *, n_prompt, seq, t_new):
    n_tok = x.shape[0]
    nbatch = n_prompt // seq
    cb0 = 2 * A_QF + 2 * A_IV
    views = [(cb0 + 3 * B_W * grp, 3 * B_W, dil) for grp, (_, dil) in enumerate(DIL_GROUPS) if dil > 1]
    u, *u_res = _matmul(x, w_in.astype(BF16), residue_views=views)
    lb = jnp.cumsum(jax.nn.softmax(lb_logits.astype(F32), axis=0), axis=0)[0].reshape(1, A_QF)
    ng = norm_g.reshape(1, HGRN_VDIM)
    zeros = jnp.zeros((nbatch, HGRN_HEADS, HGRN_KDIM, HGRN_VDIM), F32)
    gated_p, hgrn_p = _hgrn(u, lb, ng, zeros, row0=0, seq=seq, tb=512, c=HGRN_CHUNK)
    gated_s, hgrn_s = _hgrn(u, lb, ng, state_hgrn, row0=n_prompt, seq=t_new, tb=t_new, c=t_new)
    os_, lses, swa_p, swa_s = [], [], [], []
    u_res = iter(u_res)
    for grp, (win, dil) in enumerate(DIL_GROUPS):
        if dil > 1:
            o_p, lse_p = _dil_prompt(next(u_res), grp, n_prompt=n_prompt, seq=seq, ncb=3, cb0=0)
        else:
            o_p, lse_p = _dil_prompt(u, grp, n_prompt=n_prompt, seq=seq, ncb=EVEN_IN // B_W,
                                     cb0=cb0 // B_W + 3 * grp)
        nbuf, o_s, lse_s = _dil_sample(u, caches[grp], grp, n_prompt=n_prompt, t_new=t_new)
        os_.append((o_p, o_s))
        lses.append((lse_p, lse_s))
        swa_s.append(nbuf)
        kv = u[:n_prompt, cb0 + 3 * B_W * grp + B_W:cb0 + 3 * B_W * (grp + 1)]
        kv = kv.reshape(nbatch, seq, 2, DIL_HEADS, DIL_HDIM)
        swa_p.append(kv[:, seq - min(win, seq):])
    y = _even_out((gated_p, gated_s), os_, lses, x, w_out.astype(BF16), ln_g.reshape(1, -1), ln_b.reshape(1, -1),
                  n_prompt=n_prompt)
    return y, hgrn_p, hgrn_s, swa_p, swa_s


HEAD_W = LANES
ROPE_LO = MLA_NOPE
ROPE_MID = MLA_NOPE + MLA_ROPE // 2
ROPE_HI = MLA_NOPE + MLA_ROPE


def _rms_rows(x, g):
    return x * lax.rsqrt(jnp.mean(x * x, axis=-1, keepdims=True) + RMS_EPS) * g


def _odd_prep_kernel(u_ref, cos_ref, sin_ref, gq_ref, gkv_ref, wqa_ref, wqb_ref, wk_ref, wv_ref,
                     lat_ref, q_ref, k_ref, v_ref):
    ckv = u_ref[:, 0:MLA_KV_LORA]
    krb = u_ref[:, MLA_KV_LORA:MLA_KV_LORA + HEAD_W]
    cq = u_ref[:, MLA_KV_LORA + HEAD_W:MLA_KV_LORA + HEAD_W + MLA_Q_LORA]
    cos = cos_ref[...]
    sin = sin_ref[...]
    cos_h = jnp.concatenate([cos] * MLA_HEADS, axis=1)
    sin_h = jnp.concatenate([sin] * MLA_HEADS, axis=1)
    cqn = _rms_rows(cq, gq_ref[...]).astype(BF16)
    q_ref[...] = (_nn(cqn, wqa_ref[...]) * cos_h + _nn(cqn, wqb_ref[...]) * sin_h).astype(BF16)
    c = _rms_rows(ckv, gkv_ref[...])
    lane = lax.broadcasted_iota(jnp.int32, krb.shape, 1)
    swapped = jnp.where(lane < ROPE_MID, pltpu.roll(krb, HEAD_W - MLA_ROPE // 2, 1), pltpu.roll(krb, MLA_ROPE // 2, 1))
    rot = krb * cos + swapped * sin
    lat_ref[:, 0:MLA_KV_LORA] = c
    lat_ref[:, MLA_KV_LORA:MLA_LAT] = rot[:, ROPE_LO:ROPE_HI]
    cb = c.astype(BF16)
    k_ref[...] = (_nn(cb, wk_ref[...]) + jnp.concatenate([rot] * MLA_HEADS, axis=1)).astype(BF16)
    v_ref[...] = _nn(cb, wv_ref[...]).astype(BF16)


def _odd_prep(u_a, cos_t, sin_t, gq, gkv, wqa, wqb, wk, wv, *, tm=256):
    n = u_a.shape[0]
    row = lambda w_: pl.BlockSpec((tm, w_), lambda i: (i, 0))
    const = lambda a: pl.BlockSpec(a.shape, lambda i: (0, 0))
    hw = MLA_HEADS * HEAD_W
    return pl.pallas_call(
        _odd_prep_kernel,
        grid=(n // tm,),
        in_specs=[row(u_a.shape[1]), row(HEAD_W), row(HEAD_W), const(gq), const(gkv),
                  const(wqa), const(wqb), const(wk), const(wv)],
        out_specs=[row(MLA_LAT), row(hw), row(hw), row(MLA_HEADS * MLA_VDIM)],
        out_shape=[jax.ShapeDtypeStruct((n, MLA_LAT), F32), jax.ShapeDtypeStruct((n, hw), BF16),
                   jax.ShapeDtypeStruct((n, hw), BF16), jax.ShapeDtypeStruct((n, MLA_HEADS * MLA_VDIM), BF16)],
        compiler_params=_params(("parallel",), 40),
        name="odd_prep",
    )(u_a, cos_t, sin_t, gq, gkv, wqa, wqb, wk, wv)


def _odd_weights(w_in, w_uq, w_uk, w_uv):
    o = 0
    cq_w = w_in[:, o:o + MLA_Q_LORA]; o += MLA_Q_LORA
    ckv_w = w_in[:, o:o + MLA_KV_LORA]; o += MLA_KV_LORA
    kr_w = w_in[:, o:o + MLA_ROPE]; o += MLA_ROPE
    rest = w_in[:, o:]
    zeros = lambda w_: jnp.zeros((D_MODEL, w_), w_in.dtype)
    w_in2 = jnp.concatenate([ckv_w, zeros(ROPE_LO), kr_w, zeros(HEAD_W - ROPE_HI), cq_w, rest], axis=1)

    half = MLA_ROPE // 2
    nope = w_uq[:, :, :MLA_NOPE]
    x1 = w_uq[:, :, MLA_NOPE:MLA_NOPE + half]
    x2 = w_uq[:, :, MLA_NOPE + half:]
    pad = jnp.zeros((MLA_Q_LORA, MLA_HEADS, HEAD_W - ROPE_HI), w_uq.dtype)
    wqa = jnp.concatenate([nope, x1, x2, pad], axis=2).reshape(MLA_Q_LORA, MLA_HEADS * HEAD_W)
    wqb = jnp.concatenate([jnp.zeros_like(nope), x2, x1, pad], axis=2).reshape(MLA_Q_LORA, MLA_HEADS * HEAD_W)
    kpad = jnp.zeros((MLA_KV_LORA, MLA_HEADS, HEAD_W - MLA_NOPE), w_uk.dtype)
    wk = jnp.concatenate([w_uk, kpad], axis=2).reshape(MLA_KV_LORA, MLA_HEADS * HEAD_W)
    wv = w_uv.reshape(MLA_KV_LORA, MLA_HEADS * MLA_VDIM)
    up = jnp.zeros((MLA_HEADS, HEAD_W, MLA_LAT), F32)
    up = up.at[:, :MLA_NOPE, :MLA_KV_LORA].set(jnp.transpose(w_uk, (1, 2, 0)))
    up = up.at[:, ROPE_LO:ROPE_HI, MLA_KV_LORA:].set(jnp.broadcast_to(jnp.eye(MLA_ROPE, dtype=F32), (MLA_HEADS, MLA_ROPE, MLA_ROPE)))
    wvh = jnp.transpose(w_uv, (1, 0, 2))
    return (w_in2.astype(BF16), wqa.astype(BF16), wqb.astype(BF16), wk.astype(BF16), wv.astype(BF16),
            up.astype(BF16), wvh.astype(BF16))


def _rope_tables(pos):
    half = MLA_ROPE // 2
    inv = ROPE_THETA ** (-jnp.arange(half, dtype=F32) / half)
    ang = pos.astype(F32)[:, None] * inv[None, :]
    cos, sin = jnp.cos(ang), jnp.sin(ang)
    n = pos.shape[0]
    cos_t = jnp.concatenate([jnp.ones((n, ROPE_LO), F32), cos, cos, jnp.zeros((n, HEAD_W - ROPE_HI), F32)], axis=1)
    sin_t = jnp.concatenate([jnp.zeros((n, ROPE_LO), F32), -sin, sin, jnp.zeros((n, HEAD_W - ROPE_HI), F32)], axis=1)
    return cos_t, sin_t


BIG_NEG = -1e30


def _causal_tiles(nq):
    qs = [i for i in range(nq) for _ in range(i + 1)]
    ks = [j for i in range(nq) for j in range(i + 1)]
    return jnp.asarray(qs, jnp.int32), jnp.asarray(ks, jnp.int32)


def _mla_flash_kernel(qt_ref, kt_ref, q_ref, k_ref, v_ref, o_ref, m_scr, l_scr, acc_scr):
    step = pl.program_id(2)
    qi = qt_ref[step]
    ki = kt_ref[step]
    tq, tk = q_ref.shape[0], k_ref.shape[0]
    scale = (MLA_NOPE + MLA_ROPE) ** -0.5

    @pl.when(ki == 0)
    def _():
        m_scr[...] = jnp.full(m_scr.shape, BIG_NEG, F32)
        l_scr[...] = jnp.zeros(l_scr.shape, F32)
        acc_scr[...] = jnp.zeros(acc_scr.shape, F32)

    nfold = tk // LANES

    def update(masked):
        v = v_ref[...]
        if masked:
            row = lax.broadcasted_iota(jnp.int32, (tq, tk), 0)
            col = lax.broadcasted_iota(jnp.int32, (tq, tk), 1)
            keep = col <= row
        for hh in range(2):
            cs = slice(HEAD_W * hh, HEAD_W * (hh + 1))
            s = _nt(q_ref[:, cs], k_ref[:, cs]) * scale
            if masked:
                s = jnp.where(keep, s, BIG_NEG)
            s_fold = s[:, 0:LANES]
            for i in range(1, nfold):
                s_fold = jnp.maximum(s_fold, s[:, i * LANES:(i + 1) * LANES])
            m_old = m_scr[hh]
            m_new = jnp.maximum(m_old, jnp.max(s_fold, axis=-1, keepdims=True))
            alpha = jnp.exp(m_old - m_new)
            p = jnp.exp(s - jnp.concatenate([m_new] * nfold, axis=1))
            p_fold = p[:, 0:LANES]
            for i in range(1, nfold):
                p_fold = p_fold + p[:, i * LANES:(i + 1) * LANES]
            l_scr[hh] = alpha * l_scr[hh] + p_fold
            acc_scr[hh] = alpha * acc_scr[hh] + _nn(p.astype(BF16), v)
            m_scr[hh] = m_new

    @pl.when(ki < qi)
    def _():
        update(False)

    @pl.when(ki == qi)
    def _():
        update(True)
        lane = lax.broadcasted_iota(jnp.int32, (tq, 2 * MLA_VDIM), 1)
        l0 = jnp.sum(l_scr[0], axis=-1, keepdims=True)
        l1 = jnp.sum(l_scr[1], axis=-1, keepdims=True)
        o = jnp.where(lane < MLA_VDIM, acc_scr[0] / l0, acc_scr[1] / l1)
        o_ref[...] = o.astype(o_ref.dtype)


def _mla_prompt(q, k, v, *, n_prompt, seq, tq=512):
    nbatch = n_prompt // seq
    nq = seq // tq
    npair = MLA_HEADS // 2
    qt, kt = _causal_tiles(nq)
    grid_spec = pltpu.PrefetchScalarGridSpec(
        num_scalar_prefetch=2,
        grid=(nbatch, npair, qt.shape[0]),
        in_specs=[pl.BlockSpec((tq, 2 * HEAD_W), lambda b, h, s, qt, kt: (b * nq + qt[s], h)),
                  pl.BlockSpec((tq, 2 * HEAD_W), lambda b, h, s, qt, kt: (b * nq + kt[s], h)),
                  pl.BlockSpec((tq, 2 * MLA_VDIM), lambda b, h, s, qt, kt: (b * nq + kt[s], h))],
        out_specs=pl.BlockSpec((tq, 2 * MLA_VDIM), lambda b, h, s, qt, kt: (b * nq + qt[s], h)),
        scratch_shapes=[pltpu.VMEM((2, tq, LANES), F32), pltpu.VMEM((2, tq, LANES), F32),
                        pltpu.VMEM((2, tq, 2 * MLA_VDIM), F32)],
    )
    return pl.pallas_call(
        _mla_flash_kernel,
        grid_spec=grid_spec,
        out_shape=jax.ShapeDtypeStruct((n_prompt, MLA_HEADS * MLA_VDIM), BF16),
        compiler_params=_params(("parallel", "parallel", "arbitrary"), 32),
        name="mla_prompt_attention",
    )(qt, kt, q, k, v)


def _absorb_kernel(q_ref, up_ref, o_ref):
    for h in range(MLA_HEADS):
        o_ref[h] = _nn(q_ref[:, HEAD_W * h:HEAD_W * (h + 1)], up_ref[h])


def _absorb(q, up, *, row0, nrows, tm=128):
    assert row0 % tm == 0 and nrows % tm == 0
    return pl.pallas_call(
        _absorb_kernel,
        grid=(nrows // tm,),
        in_specs=[pl.BlockSpec((tm, MLA_HEADS * HEAD_W), lambda i: (row0 // tm + i, 0)),
                  pl.BlockSpec(up.shape, lambda i: (0, 0, 0))],
        out_specs=pl.BlockSpec((MLA_HEADS, tm, MLA_LAT), lambda i: (0, i, 0)),
        out_shape=jax.ShapeDtypeStruct((MLA_HEADS, nrows, MLA_LAT), F32),
        compiler_params=_params(("parallel",), 32),
        name="mla_absorb_query",
    )(q, up)


def _page_copy(cache_hbm, buf, sem, page, slot, j):
    return pltpu.make_async_copy(cache_hbm.at[page], buf.at[slot, j], sem.at[slot])


class _PageStream:
    def __init__(self, pt_ref, cache_hbm, buf, sem, *, pg, nchunk):
        self.pt_ref, self.cache_hbm, self.buf, self.sem = pt_ref, cache_hbm, buf, sem
        self.pg, self.nchunk = pg, nchunk
        self.b = pl.program_id(0)
        self.nb = pl.num_programs(0)

    def slot(self, c):
        return (self.b * self.nchunk + c) % 2

    def _start(self, seq, chunk, slot):
        for j in range(self.pg):
            _page_copy(self.cache_hbm, self.buf, self.sem, self.pt_ref[seq, chunk * self.pg + j], slot, j
                       ).start(priority=j % 2)

    def prologue(self):
        @pl.when(self.b == 0)
        def _():
            self._start(0, 0, 0)

    def acquire(self, c):
        slot = self.slot(c)

        @pl.when(c + 1 < self.nchunk)
        def _():
            self._start(self.b, c + 1, 1 - slot)

        @pl.when((c + 1 == self.nchunk) & (self.b + 1 < self.nb))
        def _():
            self._start(self.b + 1, 0, 1 - slot)

        for j in range(self.pg):
            _page_copy(self.cache_hbm, self.buf, self.sem, 0, slot, j).wait()
        return slot


def _mla_sample_kernel(pt_ref, q_ref, new_ref, cache_hbm, o_ref, buf, sem, *, pg, nchunk):
    t_new = new_ref.shape[0]
    nrow = MLA_HEADS * t_new
    scale = (MLA_NOPE + MLA_ROPE) ** -0.5
    stream = _PageStream(pt_ref, cache_hbm, buf, sem, pg=pg, nchunk=nchunk)
    stream.prologue()
    q = q_ref[...].reshape(nrow, MLA_LAT).astype(BF16)

    def body(c, carry):
        m, l, acc = carry
        slot = stream.acquire(c)
        kc = jnp.concatenate([buf[slot, j].astype(BF16) for j in range(pg)], axis=1)
        s = _nn(q, kc) * scale
        smax = s[:, 0:PAGE_SIZE]
        for j in range(1, pg):
            smax = jnp.maximum(smax, s[:, j * PAGE_SIZE:(j + 1) * PAGE_SIZE])
        m_new = jnp.maximum(m, jnp.max(smax, axis=-1, keepdims=True))
        alpha = jnp.exp(m - m_new)
        p = jnp.exp(s - m_new)
        psum = p[:, 0:PAGE_SIZE]
        for j in range(1, pg):
            psum = psum + p[:, j * PAGE_SIZE:(j + 1) * PAGE_SIZE]
        l = alpha * l + jnp.sum(psum, axis=-1, keepdims=True)
        acc = alpha * acc + _nt(p.astype(BF16), kc[0:MLA_KV_LORA, :])
        return m_new, l, acc

    init = (jnp.full((nrow, 1), BIG_NEG, F32), jnp.zeros((nrow, 1), F32), jnp.zeros((nrow, MLA_KV_LORA), F32))
    m, l, acc = lax.fori_loop(0, nchunk, body, init)

    kn = new_ref[...].astype(BF16)
    tq = lax.broadcasted_iota(jnp.int32, (nrow, t_new), 0) % t_new
    tk = lax.broadcasted_iota(jnp.int32, (nrow, t_new), 1)
    s = jnp.where(tk <= tq, _nt(q, kn) * scale, BIG_NEG)
    m_new = jnp.maximum(m, jnp.max(s, axis=-1, keepdims=True))
    alpha = jnp.exp(m - m_new)
    p = jnp.exp(s - m_new)
    l = alpha * l + jnp.sum(p, axis=-1, keepdims=True)
    acc = alpha * acc + _nn(p.astype(BF16), kn[:, 0:MLA_KV_LORA])
    o_ref[...] = (acc / l).reshape(MLA_HEADS, t_new, MLA_KV_LORA)


def _mla_sample(q_abs, lat, cache, page_table, *, row0, t_new, pg=16):
    nb, npages = page_table.shape
    assert npages % pg == 0 and t_new == SUBLANES and row0 % t_new == 0
    cache = jnp.transpose(cache, (0, 2, 1))
    grid_spec = pltpu.PrefetchScalarGridSpec(
        num_scalar_prefetch=1,
        grid=(nb,),
        in_specs=[pl.BlockSpec((MLA_HEADS, t_new, MLA_LAT), lambda b, pt: (0, b, 0)),
                  pl.BlockSpec((t_new, MLA_LAT), lambda b, pt: (row0 // t_new + b, 0)),
                  pl.BlockSpec(memory_space=pl.ANY)],
        out_specs=pl.BlockSpec((MLA_HEADS, t_new, MLA_KV_LORA), lambda b, pt: (0, b, 0)),
        scratch_shapes=[pltpu.VMEM((2, pg, MLA_LAT, PAGE_SIZE), F32), pltpu.SemaphoreType.DMA((2,))],
    )
    return pl.pallas_call(
        functools.partial(_mla_sample_kernel, pg=pg, nchunk=npages // pg),
        grid_spec=grid_spec,
        out_shape=jax.ShapeDtypeStruct((MLA_HEADS, nb * t_new, MLA_KV_LORA), F32),
        compiler_params=_params(("arbitrary",), 40),
        name="mla_sample_attention",
    )(page_table, q_abs, lat, cache)


def _latent_up_kernel(o_ref, w_ref, y_ref):
    outs = [_nn(o_ref[h].astype(BF16), w_ref[h]) for h in range(MLA_HEADS)]
    y_ref[...] = jnp.concatenate(outs, axis=1).astype(y_ref.dtype)


def _latent_up(o_lat, wvh, *, tm=128):
    nrows = o_lat.shape[1]
    return pl.pallas_call(
        _latent_up_kernel,
        grid=(nrows // tm,),
        in_specs=[pl.BlockSpec((MLA_HEADS, tm, MLA_KV_LORA), lambda i: (0, i, 0)),
                  pl.BlockSpec(wvh.shape, lambda i: (0, 0, 0))],
        out_specs=pl.BlockSpec((tm, MLA_HEADS * MLA_VDIM), lambda i: (i, 0)),
        out_shape=jax.ShapeDtypeStruct((nrows, MLA_HEADS * MLA_VDIM), BF16),
        compiler_params=_params(("parallel",), 32),
        name="mla_latent_up",
    )(o_lat, wvh)


def _block_mean_kernel(k_ref, o_ref, *, nblk):
    k = k_ref[...].reshape(nblk, MOBA_BLOCK, D_KV)
    o_ref[0, 0:nblk, :] = jnp.mean(k, axis=1)
    if nblk < o_ref.shape[1]:
        o_ref[0, nblk:, :] = jnp.zeros((o_ref.shape[1] - nblk, D_KV), F32)


def _block_means(u_b, *, n_prompt, seq):
    nbatch = n_prompt // seq
    nblk = seq // MOBA_BLOCK
    assert seq % MOBA_BLOCK == 0 and nblk <= LANES
    return pl.pallas_call(
        functools.partial(_block_mean_kernel, nblk=nblk),
        grid=(nbatch,),
        in_specs=[pl.BlockSpec((seq, D_KV), lambda b: (b, D_Q // D_KV))],
        out_specs=pl.BlockSpec((1, LANES, D_KV), lambda b: (b, 0, 0)),
        out_shape=jax.ShapeDtypeStruct((nbatch, LANES, D_KV), F32),
        compiler_params=_params(("parallel",), 32),
        name="moba_block_means",
    )(u_b)


def _top_blocks(gate, limit):
    lane = lax.broadcasted_iota(jnp.int32, gate.shape, 1)
    g = jnp.where(lane < limit, gate, NEG_INF)
    sel = jnp.zeros(gate.shape, F32)
    for j in range(MOBA_TOPK):
        mx = jnp.max(g, axis=-1, keepdims=True)
        idx = jnp.min(jnp.where(g == mx, lane, LANES), axis=-1, keepdims=True)
        hit = lane == idx + jnp.where(j < limit, 0, 2 * LANES)
        sel = jnp.where(hit, 1.0, sel)
        g = jnp.where(lane == idx, NEG_INF, g)
    return sel


def _moba_flash_kernel(qt_ref, kt_ref, q_ref, k_ref, v_ref, km_ref, o_ref, sel_scr, m_scr, l_scr, acc_scr):
    step = pl.program_id(2)
    qi = qt_ref[step]
    ki = kt_ref[step]
    tq, tk = q_ref.shape[0], k_ref.shape[0]
    scale = MOBA_HDIM ** -0.5
    grp = MOBA_HEADS // MOBA_KV_HEADS
    nh = 2 * grp

    @pl.when(ki == 0)
    def _():
        m_scr[...] = jnp.full(m_scr.shape, BIG_NEG, F32)
        l_scr[...] = jnp.zeros(l_scr.shape, F32)
        acc_scr[...] = jnp.zeros(acc_scr.shape, F32)
        for hh in range(nh):
            kv = hh // grp
            qh = q_ref[:, MOBA_HDIM * hh:MOBA_HDIM * (hh + 1)].astype(BF16)
            km = km_ref[0, :, MOBA_HDIM * kv:MOBA_HDIM * (kv + 1)].astype(BF16)
            sel_scr[hh] = _top_blocks(_nt(qh, km), qi)

    nhalf = tk // LANES
    masked = 2.0 * BIG_NEG

    def update(diag):
        if diag:
            row = lax.broadcasted_iota(jnp.int32, (tq, tk), 0)
            col = lax.broadcasted_iota(jnp.int32, (tq, tk), 1)
            keep_all = col <= row
        else:
            spread = jnp.where(lax.broadcasted_iota(jnp.int32, (LANES, LANES), 0) == ki, 1.0, 0.0).astype(BF16)
        for hh in range(nh):
            kv = hh // grp
            ks = slice(MOBA_HDIM * kv, MOBA_HDIM * (kv + 1))
            qh = (q_ref[:, MOBA_HDIM * hh:MOBA_HDIM * (hh + 1)] * scale).astype(BF16)
            s = _nt(qh, k_ref[:, ks].astype(BF16))
            if diag:
                keep = keep_all
            else:
                picked = _nn(sel_scr[hh].astype(BF16), spread)
                keep = jnp.concatenate([picked] * nhalf, axis=1) > 0.5
            s = jnp.where(keep, s, masked)
            s_fold = s[:, 0:LANES]
            for i in range(1, nhalf):
                s_fold = jnp.maximum(s_fold, s[:, i * LANES:(i + 1) * LANES])
            m_old = m_scr[hh]
            m_new = jnp.maximum(m_old, jnp.max(s_fold, axis=-1, keepdims=True))
            alpha = jnp.exp(m_old - m_new)
            p = jnp.exp(s - jnp.concatenate([m_new] * nhalf, axis=1))
            p_fold = p[:, 0:LANES]
            for i in range(1, nhalf):
                p_fold = p_fold + p[:, i * LANES:(i + 1) * LANES]
            l_scr[hh] = alpha * l_scr[hh] + p_fold
            acc_scr[hh] = alpha[:, 0:MOBA_HDIM] * acc_scr[hh] + _nn(p.astype(BF16), v_ref[:, ks].astype(BF16))
            m_scr[hh] = m_new

    @pl.when(ki < qi)
    def _():
        update(False)

    @pl.when(ki == qi)
    def _():
        update(True)
        outs = [acc_scr[hh] / jnp.sum(l_scr[hh], axis=-1, keepdims=True) for hh in range(nh)]
        o_ref[...] = jnp.concatenate(outs, axis=1).astype(o_ref.dtype)


def _moba_prompt(u_b, kmean, *, n_prompt, seq):
    nbatch = n_prompt // seq
    tq = MOBA_BLOCK
    nq = seq // tq
    assert math.frexp(MOBA_HDIM ** -0.5)[0] == 0.5, "score scale must be a power of two"
    npair = MOBA_KV_HEADS // 2
    grp = MOBA_HEADS // MOBA_KV_HEADS
    qw = 2 * grp * MOBA_HDIM
    kw = 2 * MOBA_HDIM
    kcb = D_Q // kw
    vcb = (D_Q + D_KV) // kw
    qt, kt = _causal_tiles(nq)
    grid_spec = pltpu.PrefetchScalarGridSpec(
        num_scalar_prefetch=2,
        grid=(nbatch, npair, qt.shape[0]),
        in_specs=[pl.BlockSpec((tq, qw), lambda b, h, s, qt, kt: (b * nq + qt[s], h)),
                  pl.BlockSpec((tq, kw), lambda b, h, s, qt, kt: (b * nq + kt[s], kcb + h)),
                  pl.BlockSpec((tq, kw), lambda b, h, s, qt, kt: (b * nq + kt[s], vcb + h)),
                  pl.BlockSpec((1, LANES, kw), lambda b, h, s, qt, kt: (b, 0, h))],
        out_specs=pl.BlockSpec((tq, qw), lambda b, h, s, qt, kt: (b * nq + qt[s], h)),
        scratch_shapes=[pltpu.VMEM((2 * grp, tq, LANES), F32), pltpu.VMEM((2 * grp, tq, LANES), F32),
                        pltpu.VMEM((2 * grp, tq, LANES), F32), pltpu.VMEM((2 * grp, tq, MOBA_HDIM), F32)],
    )
    return pl.pallas_call(
        _moba_flash_kernel,
        grid_spec=grid_spec,
        out_shape=jax.ShapeDtypeStruct((n_prompt, D_Q), BF16),
        compiler_params=_params(("parallel", "parallel", "arbitrary"), 32),
        name="moba_prompt_attention",
    )(qt, kt, u_b, u_b, u_b, kmean)


def _moba_sample_kernel(pt_ref, q_ref, kvn_ref, cache_hbm, o_ref, buf, sem, km_scr, m_scr, l_scr, o_scr,
                        *, pg, nchunk):
    t_new = q_ref.shape[0]
    grp = MOBA_HEADS // MOBA_KV_HEADS
    nrow = MOBA_HEADS * t_new
    scale = MOBA_HDIM ** -0.5
    bpc = pg * PAGE_SIZE // MOBA_BLOCK
    nblk = nchunk * bpc
    stream = _PageStream(pt_ref, cache_hbm, buf, sem, pg=pg, nchunk=nchunk)
    stream.prologue()
    q = q_ref[...]
    pieces = []
    for h in range(MOBA_HEADS):
        kv = h // grp
        parts = []
        if kv > 0:
            parts.append(jnp.zeros((t_new, MOBA_HDIM * kv), F32))
        parts.append(q[:, MOBA_HDIM * h:MOBA_HDIM * (h + 1)])
        if kv < MOBA_KV_HEADS - 1:
            parts.append(jnp.zeros((t_new, MOBA_HDIM * (MOBA_KV_HEADS - 1 - kv)), F32))
        pieces.append(jnp.concatenate(parts, axis=1))
    qbd = jnp.concatenate(pieces, axis=0).astype(BF16)
    km_scr[...] = jnp.zeros(km_scr.shape, F32)
    m_scr[...] = jnp.full(m_scr.shape, BIG_NEG, F32)
    l_scr[...] = jnp.zeros(l_scr.shape, F32)
    lane_k = lax.broadcasted_iota(jnp.int32, (D_KV, LANES), 1)
    lane_r = lax.broadcasted_iota(jnp.int32, (nrow, LANES), 1)

    def body(c, carry):
        slot = stream.acquire(c)
        ppb = MOBA_BLOCK // PAGE_SIZE
        kt = jnp.concatenate([buf[slot, jj, 0:D_KV, :].astype(BF16) for jj in range(pg)], axis=1)
        s_all = _nn(qbd, kt) * scale
        for j in range(bpc):
            n = c * bpc + j
            ksum = buf[slot, j * ppb, 0:D_KV, :]
            for i in range(1, ppb):
                ksum = ksum + buf[slot, j * ppb + i, 0:D_KV, :]
            kmean = jnp.sum(ksum, axis=1, keepdims=True) * (1.0 / MOBA_BLOCK)
            km_scr[...] = jnp.where(lane_k == n, kmean, km_scr[...])
            sj = s_all[:, j * MOBA_BLOCK:(j + 1) * MOBA_BLOCK]
            s_fold = sj[:, 0:LANES]
            for i in range(1, ppb):
                s_fold = jnp.maximum(s_fold, sj[:, i * LANES:(i + 1) * LANES])
            m = jnp.max(s_fold, axis=-1, keepdims=True)
            p = jnp.exp(sj - m)
            p_fold = p[:, 0:LANES]
            for i in range(1, ppb):
                p_fold = p_fold + p[:, i * LANES:(i + 1) * LANES]
            vt = jnp.concatenate([buf[slot, j * ppb + i, D_KV:2 * D_KV, :].astype(BF16) for i in range(ppb)],
                                 axis=1)
            m_scr[...] = jnp.where(lane_r == n, m, m_scr[...])
            l_scr[...] = jnp.where(lane_r == n, jnp.sum(p_fold, axis=-1, keepdims=True), l_scr[...])
            o_scr[n] = _nt(p.astype(BF16), vt)
        return carry

    lax.fori_loop(0, nchunk, body, 0)

    kn = kvn_ref[:, 0:D_KV].astype(BF16)
    vn = kvn_ref[:, D_KV:2 * D_KV].astype(BF16)
    tq = lax.broadcasted_iota(jnp.int32, (nrow, t_new), 0) % t_new
    tk = lax.broadcasted_iota(jnp.int32, (nrow, t_new), 1)
    s_own = jnp.where(tk <= tq, _nt(qbd, kn) * scale, BIG_NEG)
    m_own = jnp.max(s_own, axis=-1, keepdims=True)
    p_own = jnp.exp(s_own - m_own)
    l_own = jnp.sum(p_own, axis=-1, keepdims=True)
    o_own = _nn(p_own.astype(BF16), vn)

    picked = _top_blocks(_nn(qbd, km_scr[...].astype(BF16)), nblk) > 0.5
    m_all = m_scr[...]
    mx = jnp.maximum(jnp.max(jnp.where(picked, m_all, BIG_NEG), axis=-1, keepdims=True), m_own)
    w_all = jnp.where(picked, jnp.exp(m_all - mx), 0.0)
    w_own = jnp.exp(m_own - mx)
    l = jnp.sum(w_all * l_scr[...], axis=-1, keepdims=True) + w_own * l_own
    o = w_own * o_own
    for n in range(nblk):
        o = o + w_all[:, n:n + 1] * o_scr[n]
    o = o / l
    outs = []
    for h in range(MOBA_HEADS):
        kv = h // grp
        outs.append(o[h * t_new:(h + 1) * t_new, MOBA_HDIM * kv:MOBA_HDIM * (kv + 1)])
    o_ref[...] = jnp.concatenate(outs, axis=1)


def _moba_sample(u_b, cache, page_table, *, row0, t_new, pg=8):
    nb, npages = page_table.shape
    past = npages * PAGE_SIZE
    assert npages % pg == 0 and past % MOBA_BLOCK == 0 and (pg * PAGE_SIZE) % MOBA_BLOCK == 0
    nblk = past // MOBA_BLOCK
    assert nblk <= LANES and t_new == SUBLANES
    nrow = MOBA_HEADS * t_new
    cachev = jnp.transpose(cache, (0, 2, 3, 4, 1)).reshape(cache.shape[0], 2 * D_KV, PAGE_SIZE)
    rb0 = row0 // t_new
    grid_spec = pltpu.PrefetchScalarGridSpec(
        num_scalar_prefetch=1,
        grid=(nb,),
        in_specs=[pl.BlockSpec((t_new, D_Q), lambda b, pt: (rb0 + b, 0)),
                  pl.BlockSpec((t_new, 2 * D_KV), lambda b, pt: (rb0 + b, D_Q // (2 * D_KV))),
                  pl.BlockSpec(memory_space=pl.ANY)],
        out_specs=pl.BlockSpec((t_new, D_Q), lambda b, pt: (b, 0)),
        scratch_shapes=[pltpu.VMEM((2, pg, 2 * D_KV, PAGE_SIZE), F32), pltpu.SemaphoreType.DMA((2,)),
                        pltpu.VMEM((D_KV, LANES), F32),
                        pltpu.VMEM((nrow, LANES), F32), pltpu.VMEM((nrow, LANES), F32),
                        pltpu.VMEM((nblk, nrow, D_KV), F32)],
    )
    return pl.pallas_call(
        functools.partial(_moba_sample_kernel, pg=pg, nchunk=npages // pg),
        grid_spec=grid_spec,
        out_shape=jax.ShapeDtypeStruct((nb * t_new, D_Q), F32),
        compiler_params=_params(("arbitrary",), 40),
        name="moba_sample_attention",
    )(page_table, u_b, u_b, cachev)


def _odd_out_kernel(ocp_ref, ocs_ref, odp_ref, ods_ref, x_ref, w_ref, g_ref, b_ref, y_ref, *, npb):
    half = MLA_HEADS * MLA_VDIM

    def compute(oc_ref, od_ref):
        m = (_nn(oc_ref[...].astype(BF16), w_ref[0:half, :])
             + _nn(od_ref[...].astype(BF16), w_ref[half:half + D_Q, :]))
        y_ref[...] = _layer_norm_rows(DN_ALPHA * x_ref[...] + m, g_ref[...], b_ref[...])

    i = pl.program_id(0)

    @pl.when(i < npb)
    def _():
        compute(ocp_ref, odp_ref)

    @pl.when(i >= npb)
    def _():
        compute(ocs_ref, ods_ref)


def _odd_out(oc, od, x, w_bf16, g, b, *, n_prompt, tm=512):
    n = x.shape[0]
    assert n % tm == 0 and n_prompt % tm == 0
    npb = n_prompt // tm
    const = lambda a: pl.BlockSpec(a.shape, lambda i: (0, 0))
    in_specs = (_two_source_specs(oc[0].shape[1], tm, npb) + _two_source_specs(od[0].shape[1], tm, npb)
                + [pl.BlockSpec((tm, D_MODEL), lambda i: (i, 0)), const(w_bf16), const(g), const(b)])
    return pl.pallas_call(
        functools.partial(_odd_out_kernel, npb=npb),
        grid=(n // tm,),
        in_specs=in_specs,
        out_specs=pl.BlockSpec((tm, D_MODEL), lambda i: (i, 0)),
        out_shape=jax.ShapeDtypeStruct((n, D_MODEL), F32),
        compiler_params=_params(("arbitrary",), 40),
        name="odd_out_deepnorm",
    )(*oc, *od, x, w_bf16, g, b)


def _odd_layer(x, cache_mla, cache_moba, page_table, w_in, gq, w_uq, gkv, w_uk, w_uv, w_out, ln_g, ln_b,
               *, n_prompt, seq, t_new):
    n_tok = x.shape[0]
    n_sample = n_tok - n_prompt
    past = page_table.shape[1] * PAGE_SIZE
    w_in2, wqa, wqb, wk, wv, up, wvh = _odd_weights(w_in, w_uq, w_uk, w_uv)
    wa = MLA_KV_LORA + HEAD_W + MLA_Q_LORA
    u_a, u_b = _matmul(x, w_in2, splits=(wa, D_Q + 2 * D_KV))
    pos = jnp.concatenate([jnp.tile(jnp.arange(seq, dtype=jnp.int32), n_prompt // seq),
                           jnp.tile(past + jnp.arange(t_new, dtype=jnp.int32), n_sample // t_new)])
    cos_t, sin_t = _rope_tables(pos)
    lat, q, k, v = _odd_prep(u_a, cos_t, sin_t, gq.reshape(1, -1), gkv.reshape(1, -1), wqa, wqb, wk, wv)
    oc_p = _mla_prompt(q, k, v, n_prompt=n_prompt, seq=seq)
    q_abs = _absorb(q, up, row0=n_prompt, nrows=n_sample)
    o_lat = _mla_sample(q_abs, lat, cache_mla, page_table, row0=n_prompt, t_new=t_new)
    oc_s = _latent_up(o_lat, wvh)
    kmean = _block_means(u_b, n_prompt=n_prompt, seq=seq)
    od_p = _moba_prompt(u_b, kmean, n_prompt=n_prompt, seq=seq)
    od_s = _moba_sample(u_b, cache_moba, page_table, row0=n_prompt, t_new=t_new)
    w_out_b = w_out.astype(BF16)
    g2, b2 = ln_g.reshape(1, -1), ln_b.reshape(1, -1)
    y = _odd_out((oc_p, oc_s), (od_p, od_s), x, w_out_b, g2, b2, n_prompt=n_prompt)
    kv_new = u_b[:, D_Q:]
    return y, lat, kv_new


def kernel(x_prompt, x_sample, state_hgrn, cache_swa_w128, cache_swa_w512, cache_swa_w2048, cache_mla, cache_moba_kv, page_table, w_in_even, hgrn_lb_logits, hgrn_norm_g, w_out_even, w_in_odd, mla_q_norm_g, mla_w_uq, mla_kv_norm_g, mla_w_uk, mla_w_uv, w_out_odd, ln_mix_g, ln_mix_b, ln_ffn_g, ln_ffn_b, router_group_w, router_group_b, router_expert_w, router_expert_b, expert_w_gate, expert_w_up, expert_w_down):
    nbatch, seq, _ = x_prompt.shape
    nb_s, t_new, _ = x_sample.shape
    n_prompt = nbatch * seq
    x = jnp.concatenate([x_prompt.reshape(n_prompt, D_MODEL), x_sample.reshape(nb_s * t_new, D_MODEL)], axis=0)
    caches = [cache_swa_w128, cache_swa_w512, cache_swa_w2048]
    x, hgrn_p, hgrn_s, swa_p, swa_s = _even_layer(
        x, state_hgrn, caches, w_in_even, hgrn_lb_logits, hgrn_norm_g, w_out_even, ln_mix_g[0], ln_mix_b[0],
        n_prompt=n_prompt, seq=seq, t_new=t_new)

    def moe(x, layer):
        wr, br = _router_weights(router_group_w[layer], router_group_b[layer],
                                 router_expert_w[layer], router_expert_b[layer])
        return _hier_moe(x, wr, br, expert_w_gate[layer].astype(BF16), expert_w_up[layer].astype(BF16),
                         expert_w_down[layer].astype(BF16),
                         ln_ffn_g[layer].reshape(1, -1), ln_ffn_b[layer].reshape(1, -1))

    x = moe(x, 0)
    x, lat, kv_new = _odd_layer(
        x, cache_mla, cache_moba_kv, page_table, w_in_odd, mla_q_norm_g, mla_w_uq, mla_kv_norm_g, mla_w_uk,
        mla_w_uv, w_out_odd, ln_mix_g[1], ln_mix_b[1], n_prompt=n_prompt, seq=seq, t_new=t_new)
    x = moe(x, 1)
    kv_shape = (2, MOBA_KV_HEADS, MOBA_HDIM)
    return (x[:n_prompt].reshape(x_prompt.shape), x[n_prompt:].reshape(x_sample.shape),
            hgrn_p, hgrn_s, swa_p[0], swa_s[0], swa_p[1], swa_s[1], swa_p[2], swa_s[2],
            lat[:n_prompt].reshape(nbatch, seq, MLA_LAT), lat[n_prompt:].reshape(nb_s, t_new, MLA_LAT),
            kv_new[:n_prompt].reshape((nbatch, seq) + kv_shape), kv_new[n_prompt:].reshape((nb_s, t_new) + kv_shape))
```
